```python
import math
import jax, jax.numpy as jnp
from jax import lax
import numpy as np

D_MODEL = 1024
BATCH = 8
SEQ = 4096
DEPTH = 1
DEC_BATCH = 128
DEC_SEQ = 8
PAST_LEN = 8192
PAGE_SIZE = 128

HEAD_DIM = 64
N_Q_HEADS = D_MODEL // HEAD_DIM
N_KV_HEADS = 4
Q_PER_KV = N_Q_HEADS // N_KV_HEADS
ATTN_WIDTH = N_Q_HEADS * HEAD_DIM
KV_WIDTH = N_KV_HEADS * HEAD_DIM
Q_BLOCK = 128
D_INNER = 2 * D_MODEL
SSD_HEAD_DIM = 64
SSD_HEADS = D_INNER // SSD_HEAD_DIM
N_GROUPS = 4
HEADS_PER_GROUP = SSD_HEADS // N_GROUPS
D_STATE = 128
CONV_W = 4
CONV_CH = D_INNER + 2 * N_GROUPS * D_STATE
SSD_CHUNK = 128
D_FF = -(-8 * D_MODEL // (3 * 256)) * 256
EPS = 1e-6
IN_SIZES = (ATTN_WIDTH, KV_WIDTH, KV_WIDTH, D_INNER, CONV_CH, SSD_HEADS, D_MODEL, D_MODEL)
IN_WIDTH = sum(IN_SIZES)

kernel_name = "stickbreak_ssd_gated_hybrid_step"


def _split_points(sizes):
    return np.cumsum(np.array(sizes))[:-1].tolist()


def rmsnorm(x, w):
    xf = x.astype(jnp.float32)
    y = xf * lax.rsqrt(jnp.mean(xf * xf, axis=-1, keepdims=True) + EPS) * w.astype(jnp.float32)
    return y.astype(x.dtype)


def stick_breaking_block(q, k, v, sb_bias, q_pos0):
    tq, tk = q.shape[1], k.shape[1]
    z = jnp.einsum('bqhgd,bkhd->bhgqk', q, k, preferred_element_type=jnp.float32) * (HEAD_DIM ** -0.5)
    z = z + sb_bias.astype(jnp.float32)[None, :, :, None, None]
    qpos = q_pos0 + jnp.arange(tq)
    kpos = jnp.arange(tk)
    mask = kpos[None, :] < qpos[:, None]
    log_beta = jax.nn.log_sigmoid(z)
    log_1m_beta = jnp.where(mask, jax.nn.log_sigmoid(-z), 0.0)
    suffix = lax.cumsum(log_1m_beta, axis=4, reverse=True) - log_1m_beta
    weights = jnp.where(mask, jnp.exp(log_beta + suffix), 0.0)
    return jnp.einsum('bhgqk,bkhd->bqhgd', weights.astype(v.dtype), v)


def stick_breaking_attention(q, k, v, sb_bias, q_start):
    T = q.shape[1]
    outs = []
    for s0 in range(0, T, Q_BLOCK):
        e0 = min(s0 + Q_BLOCK, T)
        lim = q_start + e0
        outs.append(stick_breaking_block(q[:, s0:e0], k[:, :lim], v[:, :lim], sb_bias, q_start + s0))
    return jnp.concatenate(outs, axis=1)


def causal_conv(xbc, conv_state, conv_w, conv_b):
    xp = jnp.concatenate([conv_state, xbc], axis=1)
    out = lax.conv_general_dilated(xp, conv_w[:, None, :].astype(xp.dtype), window_strides=(1,), padding='VALID',
                                   dimension_numbers=('NWC', 'WIO', 'NWC'), feature_group_count=CONV_CH)
    return out + conv_b, xp[:, xp.shape[1] - (CONV_W - 1):]


def ssd_chunked(x, dt, A, B, C, h0):
    b, T, H, P = x.shape
    f32 = jnp.float32
    Q = min(SSD_CHUNK, T)
    nc = -(-T // Q)
    pad = nc * Q - T
    pw4 = ((0, 0), (0, pad), (0, 0), (0, 0))
    xdt = jnp.pad(x.astype(f32) * dt[..., None], pw4).reshape(b, nc, Q, N_GROUPS, HEADS_PER_GROUP, P)
    a = jnp.pad(dt * A, ((0, 0), (0, pad), (0, 0))).reshape(b, nc, Q, N_GROUPS, HEADS_PER_GROUP)
    Bc = jnp.pad(B.astype(f32), pw4).reshape(b, nc, Q, N_GROUPS, D_STATE)
    Cc = jnp.pad(C.astype(f32), pw4).reshape(b, nc, Q, N_GROUPS, D_STATE)
    a_cum = jnp.cumsum(jnp.moveaxis(a, 2, -1), axis=-1)
    tril = jnp.tril(jnp.ones((Q, Q), dtype=bool))
    seg = jnp.where(tril, a_cum[..., :, None] - a_cum[..., None, :], -jnp.inf)
    Lmat = jnp.exp(seg)
    CB = jnp.einsum('bclgn,bcsgn->bcgls', Cc, Bc)
    y_diag = jnp.einsum('bcgls,bcgels,bcsgep->bclgep', CB, Lmat, xdt)
    decay_to_end = jnp.exp(a_cum[..., -1:] - a_cum)
    chunk_states = jnp.einsum('bclgn,bcgel,bclgep->bcgepn', Bc, decay_to_end, xdt)
    chunk_decay = jnp.exp(a_cum[..., -1])

    def step(h, inp):
        dec, st = inp
        return h * dec[..., None, None] + st, h

    h_init = h0.astype(f32).reshape(b, N_GROUPS, HEADS_PER_GROUP, P, D_STATE)
    h_last, h_prev = lax.scan(step, h_init, (jnp.moveaxis(chunk_decay, 1, 0), jnp.moveaxis(chunk_states, 1, 0)))
    h_prev = jnp.moveaxis(h_prev, 0, 1)
    y_off = jnp.einsum('bclgn,bcgepn,bcgel->bclgep', Cc, h_prev, jnp.exp(a_cum))
    y = (y_diag + y_off).reshape(b, nc * Q, H, P)[:, :T]
    return y.astype(x.dtype), h_last.reshape(b, H, P, D_STATE).astype(h0.dtype)


def hybrid_layer(x, k_past, v_past, conv_state, ssm_state, p):
    b, T, _ = x.shape
    past = k_past.shape[1]
    h = rmsnorm(x, p['norm1_w'])
    proj = h @ p['w_in']
    q, k, v, z, xbc, dt, ga, gb = jnp.split(proj, _split_points(IN_SIZES), axis=-1)
    q = rmsnorm(q.reshape(b, T, N_Q_HEADS, HEAD_DIM), p['q_norm_w']).reshape(b, T, N_KV_HEADS, Q_PER_KV, HEAD_DIM)
    k = rmsnorm(k.reshape(b, T, N_KV_HEADS, HEAD_DIM), p['k_norm_w'])
    v = v.reshape(b, T, N_KV_HEADS, HEAD_DIM)
    k_all = jnp.concatenate([k_past, k], axis=1)
    v_all = jnp.concatenate([v_past, v], axis=1)
    sb_bias = p['sb_bias'].reshape(N_KV_HEADS, Q_PER_KV)
    attn = stick_breaking_attention(q, k_all, v_all, sb_bias, past).reshape(b, T, ATTN_WIDTH)
    xbc, conv_new = causal_conv(xbc, conv_state, p['conv_w'], p['conv_b'])
    xbc = jax.nn.silu(xbc)
    xs, Bm, Cm = jnp.split(xbc, [D_INNER, D_INNER + N_GROUPS * D_STATE], axis=-1)
    xs = xs.reshape(b, T, SSD_HEADS, SSD_HEAD_DIM)
    Bm = Bm.reshape(b, T, N_GROUPS, D_STATE)
    Cm = Cm.reshape(b, T, N_GROUPS, D_STATE)
    dt = jax.nn.softplus(dt.astype(jnp.float32) + p['dt_bias'].astype(jnp.float32))
    A = -jnp.exp(p['a_log'].astype(jnp.float32))
    y, ssm_new = ssd_chunked(xs, dt, A, Bm, Cm, ssm_state)
    y = y + p['d_skip'][:, None].astype(y.dtype) * xs
    y = y.reshape(b, T, D_INNER) * jax.nn.silu(z)
    y = rmsnorm(y.reshape(b, T, N_GROUPS, D_INNER // N_GROUPS),
                p['ssd_norm_w'].reshape(N_GROUPS, D_INNER // N_GROUPS)).reshape(b, T, D_INNER)
    merged = jax.nn.sigmoid(ga) * (attn @ p['w_pa']) + jax.nn.sigmoid(gb) * (y @ p['w_pb'])
    x = x + merged @ p['w_out']
    h2 = rmsnorm(x, p['norm2_w'])
    x = x + (jax.nn.silu(h2 @ p['w_gate']) * (h2 @ p['w_up'])) @ p['w_down']
    return x, k, v, conv_new, ssm_new


def setup_inputs(seed: int = 0) -> dict:
    key = jax.random.key(seed)
    ks = jax.random.split(key, 25)
    f32 = jnp.float32
    n_pages = PAST_LEN // PAGE_SIZE
    n_used = DEC_BATCH * n_pages
    n_phys = n_used + (n_used + 3) // 4

    def nrm(k, shape, scale=1.0):
        return jax.random.normal(k, shape, f32) * scale

    x_prompt = nrm(ks[0], (BATCH, SEQ, D_MODEL))
    x_sample = nrm(ks[1], (DEC_BATCH, DEC_SEQ, D_MODEL))
    cache_k = nrm(ks[2], (DEPTH, n_phys, PAGE_SIZE, N_KV_HEADS, HEAD_DIM))
    cache_v = nrm(ks[3], (DEPTH, n_phys, PAGE_SIZE, N_KV_HEADS, HEAD_DIM))
    page_table = jax.random.permutation(ks[4], n_phys)[:n_used].reshape(DEC_BATCH, n_pages).astype(jnp.int32)
    state_ssm = nrm(ks[5], (DEPTH, DEC_BATCH, SSD_HEADS, SSD_HEAD_DIM, D_STATE), 0.3)
    state_conv = nrm(ks[6], (DEPTH, DEC_BATCH, CONV_W - 1, CONV_CH))
    norm1_w = 1.0 + nrm(ks[7], (DEPTH, D_MODEL), 0.02)
    w_in = nrm(ks[8], (DEPTH, D_MODEL, IN_WIDTH), D_MODEL ** -0.5)
    q_norm_w = 1.0 + nrm(ks[9], (DEPTH, HEAD_DIM), 0.02)
    k_norm_w = 1.0 + nrm(ks[10], (DEPTH, HEAD_DIM), 0.02)
    sb_bias = jax.random.uniform(ks[24], (DEPTH, N_Q_HEADS), f32, -8.0, -6.0)
    conv_w = nrm(ks[11], (DEPTH, CONV_W, CONV_CH), CONV_W ** -0.5)
    conv_b = nrm(ks[12], (DEPTH, CONV_CH), 0.02)
    dt0 = jnp.exp(jax.random.uniform(ks[13], (DEPTH, SSD_HEADS), f32, math.log(1e-3), math.log(1e-1)))
    dt_bias = dt0 + jnp.log(-jnp.expm1(-dt0))
    a_log = jnp.log(jax.random.uniform(ks[14], (DEPTH, SSD_HEADS), f32, 1.0, 16.0))
    d_skip = 1.0 + nrm(ks[15], (DEPTH, SSD_HEADS), 0.02)
    ssd_norm_w = 1.0 + nrm(ks[16], (DEPTH, D_INNER), 0.02)
    w_pa = nrm(ks[17], (DEPTH, ATTN_WIDTH, D_MODEL), ATTN_WIDTH ** -0.5)
    w_pb = nrm(ks[18], (DEPTH, D_INNER, D_MODEL), D_INNER ** -0.5)
    w_out = nrm(ks[19], (DEPTH, D_MODEL, D_MODEL), D_MODEL ** -0.5)
    norm2_w = 1.0 + nrm(ks[20], (DEPTH, D_MODEL), 0.02)
    w_gate = nrm(ks[21], (DEPTH, D_MODEL, D_FF), D_MODEL ** -0.5)
    w_up = nrm(ks[22], (DEPTH, D_MODEL, D_FF), D_MODEL ** -0.5)
    w_down = nrm(ks[23], (DEPTH, D_FF, D_MODEL), D_FF ** -0.5)
    return {"x_prompt": x_prompt, "x_sample": x_sample, "cache_k": cache_k, "cache_v": cache_v,
            "page_table": page_table, "state_ssm": state_ssm, "state_conv": state_conv,
            "norm1_w": norm1_w, "w_in": w_in, "q_norm_w": q_norm_w, "k_norm_w": k_norm_w,
            "sb_bias": sb_bias, "conv_w": conv_w, "conv_b": conv_b, "dt_bias": dt_bias, "a_log": a_log,
            "d_skip": d_skip, "ssd_norm_w": ssd_norm_w, "w_pa": w_pa, "w_pb": w_pb, "w_out": w_out,
            "norm2_w": norm2_w, "w_gate": w_gate, "w_up": w_up, "w_down": w_down}


def reference(x_prompt, x_sample, cache_k, cache_v, page_table, state_ssm, state_conv,
              norm1_w, w_in, q_norm_w, k_norm_w, sb_bias, conv_w, conv_b, dt_bias, a_log, d_skip,
              ssd_norm_w, w_pa, w_pb, w_out, norm2_w, w_gate, w_up, w_down):
    n_pages = page_table.shape[1]
    page = cache_k.shape[2]
    xp, xs = x_prompt, x_sample
    bp, bd = xp.shape[0], xs.shape[0]
    kp_l, vp_l, sp_l, cp_l, ks_l, vs_l, ss_l, cs_l = [], [], [], [], [], [], [], []
    for layer in range(DEPTH):
        p = dict(norm1_w=norm1_w[layer], w_in=w_in[layer], q_norm_w=q_norm_w[layer], k_norm_w=k_norm_w[layer],
                 sb_bias=sb_bias[layer], conv_w=conv_w[layer], conv_b=conv_b[layer], dt_bias=dt_bias[layer],
                 a_log=a_log[layer], d_skip=d_skip[layer], ssd_norm_w=ssd_norm_w[layer], w_pa=w_pa[layer],
                 w_pb=w_pb[layer], w_out=w_out[layer], norm2_w=norm2_w[layer], w_gate=w_gate[layer],
                 w_up=w_up[layer], w_down=w_down[layer])
        empty_kv = jnp.zeros((bp, 0, N_KV_HEADS, HEAD_DIM), xp.dtype)
        xp, k_p, v_p, c_p, s_p = hybrid_layer(
            xp, empty_kv, empty_kv, jnp.zeros((bp, CONV_W - 1, CONV_CH), xp.dtype),
            jnp.zeros((bp, SSD_HEADS, SSD_HEAD_DIM, D_STATE), xp.dtype), p)
        k_past = cache_k[layer][page_table].reshape(bd, n_pages * page, N_KV_HEADS, HEAD_DIM)
        v_past = cache_v[layer][page_table].reshape(bd, n_pages * page, N_KV_HEADS, HEAD_DIM)
        xs, k_s, v_s, c_s, s_s = hybrid_layer(xs, k_past, v_past, state_conv[layer], state_ssm[layer], p)
        kp_l.append(k_p); vp_l.append(v_p); sp_l.append(s_p); cp_l.append(c_p)
        ks_l.append(k_s); vs_l.append(v_s); ss_l.append(s_s); cs_l.append(c_s)
    return (xp, xs, jnp.stack(kp_l), jnp.stack(vp_l), jnp.stack(sp_l), jnp.stack(cp_l),
            jnp.stack(ks_l), jnp.stack(vs_l), jnp.stack(ss_l), jnp.stack(cs_l))
```

```python
import functools

import jax
import jax.numpy as jnp
import numpy as np
from jax import lax
from jax.experimental import pallas as pl
from jax.experimental.pallas import tpu as pltpu

F32 = jnp.float32
BF16 = jnp.bfloat16

D_MODEL = 1024
HEAD_DIM = 64
N_Q_HEADS = 16
N_KV_HEADS = 4
Q_PER_KV = 4
ATTN_WIDTH = 1024
KV_WIDTH = 256
D_INNER = 2048
SSD_HEAD_DIM = 64
SSD_HEADS = 32
N_GROUPS = 4
HEADS_PER_GROUP = 8
D_STATE = 128
CONV_W = 4
CONV_CH = 3072
SSD_CHUNK = 128
D_FF = 2816
EPS = 1e-6
PAGE_SIZE = 128

COL_Z, COL_Q, COL_XBC, COL_GA, COL_GB, COL_K, COL_V, COL_DT = 0, 2048, 3072, 6144, 7168, 8192, 8448, 8704
DT_PAD = 128
PROJ_W = COL_DT + DT_PAD

VMEM_LIMIT = 56 * 1024 * 1024
ATT_TQ = 256
PAGES_PER_STEP = 8


def _cparams(sem):
    return pltpu.CompilerParams(dimension_semantics=sem, vmem_limit_bytes=VMEM_LIMIT)


def _split2(x):
    hi = x.astype(BF16)
    lo = (x - hi.astype(F32)).astype(BF16)
    return hi, lo


def _softplus_parts(z):
    t = jnp.log1p(jnp.exp(-jnp.abs(z)))
    return jnp.minimum(z, 0.0) - t, jnp.minimum(-z, 0.0) - t


def _silu(x):
    return x * (1.0 / (1.0 + jnp.exp(-x)))


def _inproj_kernel(x_ref, nw_ref, w_ref, o_ref):
    x = x_ref[...]
    ms = jnp.mean(x * x, axis=-1, keepdims=True)
    xn = (x * lax.rsqrt(ms + EPS) * nw_ref[...]).astype(BF16)
    o_ref[...] = jnp.dot(xn, w_ref[...], preferred_element_type=F32)


def in_proj(x2d, norm_w, w_perm, tm=512, tn=2944):
    n, d = x2d.shape
    tm = min(tm, n)
    w = w_perm.shape[1]
    return pl.pallas_call(
        _inproj_kernel,
        grid=(w // tn, n // tm),
        in_specs=[pl.BlockSpec((tm, d), lambda j, i: (i, 0)),
                  pl.BlockSpec((1, d), lambda j, i: (0, 0)),
                  pl.BlockSpec((d, tn), lambda j, i: (0, j))],
        out_specs=pl.BlockSpec((tm, tn), lambda j, i: (i, j)),
        out_shape=jax.ShapeDtypeStruct((n, w), F32),
        compiler_params=_cparams(("parallel", "parallel")),
        name="in_proj",
    )(x2d, norm_w, w_perm)


def _qkv_kernel(q_ref, k_ref, v_ref, qw_ref, kw_ref, g_ref, qo_ref, ko_ref, vo_ref):
    gmat = g_ref[...]

    def headnorm(x, w):
        hi, lo = _split2(x * x)
        ss = jnp.dot(hi, gmat, preferred_element_type=F32) + jnp.dot(lo, gmat, preferred_element_type=F32)
        return x * lax.rsqrt(ss * (1.0 / HEAD_DIM) + EPS) * w

    qw = qw_ref[...]
    for c in range(ATTN_WIDTH // 256):
        sl = slice(c * 256, (c + 1) * 256)
        qo_ref[:, sl] = (headnorm(q_ref[:, sl], qw) * (HEAD_DIM ** -0.5)).astype(BF16)
    ko_ref[...] = headnorm(k_ref[...], kw_ref[...])
    vo_ref[...] = v_ref[...]


def qkv_post(proj, qw_t, kw_t, gmat, tm=512):
    n = proj.shape[0]
    tm = min(tm, n)
    return pl.pallas_call(
        _qkv_kernel,
        grid=(n // tm,),
        in_specs=[pl.BlockSpec((tm, ATTN_WIDTH), lambda i: (i, COL_Q // ATTN_WIDTH)),
                  pl.BlockSpec((tm, KV_WIDTH), lambda i: (i, COL_K // KV_WIDTH)),
                  pl.BlockSpec((tm, KV_WIDTH), lambda i: (i, COL_V // KV_WIDTH)),
                  pl.BlockSpec((1, 256), lambda i: (0, 0)),
                  pl.BlockSpec((1, 256), lambda i: (0, 0)),
                  pl.BlockSpec((256, 256), lambda i: (0, 0))],
        out_specs=[pl.BlockSpec((tm, ATTN_WIDTH), lambda i: (i, 0)),
                   pl.BlockSpec((tm, KV_WIDTH), lambda i: (i, 0)),
                   pl.BlockSpec((tm, KV_WIDTH), lambda i: (i, 0))],
        out_shape=[jax.ShapeDtypeStruct((n, ATTN_WIDTH), BF16),
                   jax.ShapeDtypeStruct((n, KV_WIDTH), F32),
                   jax.ShapeDtypeStruct((n, KV_WIDTH), F32)],
        compiler_params=_cparams(("parallel",)),
        name="qkv_post",
    )(proj, proj, proj, qw_t, kw_t, gmat)


def _sb_tile(z, v_bf16, umat, carry, acc, valid):
    lb, l1m = _softplus_parts(z)
    if valid is not None:
        l1m = jnp.where(valid, l1m, 0.0)
    hi, lo = _split2(l1m)
    suffix = (jnp.dot(hi, umat, preferred_element_type=F32)
              + jnp.dot(lo, umat, preferred_element_type=F32) + carry)
    w = jnp.exp(lb + suffix)
    if valid is not None:
        w = jnp.where(valid, w, 0.0)
    acc = acc + jnp.dot(w.astype(BF16), v_bf16, preferred_element_type=F32)
    carry = carry + jnp.sum(l1m, axis=-1, keepdims=True)
    return carry, acc


def _attn_prompt_kernel(bias_ref, q_ref, kt_ref, v_ref, u_ref, o_ref):
    g = pl.program_id(1)
    i = pl.program_id(2)
    tq = q_ref.shape[2]
    umat = u_ref[...]
    row = lax.broadcasted_iota(jnp.int32, (tq, tq), 0)
    col = lax.broadcasted_iota(jnp.int32, (tq, tq), 1)
    valid = col < row
    outs = []
    for j in range(Q_PER_KV):
        q = q_ref[0, j]
        bias = bias_ref[g * Q_PER_KV + j]

        def tile(jj, carry, acc, mask, q=q, bias=bias):
            z = jnp.dot(q, kt_ref[0, 0, jj], preferred_element_type=F32) + bias
            return _sb_tile(z, v_ref[0, 0, jj], umat, carry, acc, mask)

        carry = jnp.zeros((tq, 1), F32)
        acc = jnp.zeros((tq, HEAD_DIM), F32)
        carry, acc = tile(i, carry, acc, valid)

        def body(t, c, tile=tile):
            return tile(i - 1 - t, c[0], c[1], None)

        carry, acc = lax.fori_loop(0, i, body, (carry, acc))
        outs.append(acc)
    o_ref[0] = jnp.concatenate(outs, axis=-1).astype(BF16)


def attn_prompt(q_att, kt_att, v_att, sb_bias, umat):
    b, _, t, _ = q_att.shape
    tq = umat.shape[0]
    nb = t // tq
    return pl.pallas_call(
        _attn_prompt_kernel,
        grid=(b, N_KV_HEADS, nb),
        in_specs=[pl.BlockSpec(memory_space=pltpu.SMEM),
                  pl.BlockSpec((1, Q_PER_KV, tq, HEAD_DIM), lambda bb, g, i: (bb, g, i, 0)),
                  pl.BlockSpec((1, 1, nb, HEAD_DIM, tq), lambda bb, g, i: (bb, g, 0, 0, 0)),
                  pl.BlockSpec((1, 1, nb, tq, HEAD_DIM), lambda bb, g, i: (bb, g, 0, 0, 0)),
                  pl.BlockSpec((tq, tq), lambda bb, g, i: (0, 0))],
        out_specs=pl.BlockSpec((1, tq, Q_PER_KV * HEAD_DIM), lambda bb, g, i: (bb, i, g)),
        out_shape=jax.ShapeDtypeStruct((b, t, ATTN_WIDTH), BF16),
        compiler_params=_cparams(("parallel", "parallel", "parallel")),
        name="attn_prompt",
    )(sb_bias, q_att, kt_att, v_att, umat)


def _attn_sample_kernel(pt_ref, q_ref, bias_ref, kn_ref, vn_ref, *rest, t_new):
    npg = PAGES_PER_STEP
    k_refs = rest[:npg]
    v_refs = rest[npg:2 * npg]
    u_ref, o_ref, carry_ref = rest[2 * npg:]
    s = pl.program_id(1)
    q = q_ref[0]
    bias = bias_ref[...]
    umat = u_ref[...]
    m, tk = bias.shape

    def tile(k_f32, v_f32, carry, acc, valid):
        z = lax.dot_general(q, k_f32.astype(BF16), (((1,), (1,)), ((), ())),
                            preferred_element_type=F32) + bias
        return _sb_tile(z, v_f32.astype(BF16), umat, carry, acc, valid)

    @pl.when(s == 0)
    def _():
        zpad = jnp.zeros((tk - t_new, KV_WIDTH), F32)
        kn = jnp.concatenate([kn_ref[0], zpad], axis=0)
        vn = jnp.concatenate([vn_ref[0], zpad], axis=0)
        row = lax.broadcasted_iota(jnp.int32, (m, tk), 0)
        col = lax.broadcasted_iota(jnp.int32, (m, tk), 1)
        valid = col < (row % t_new)
        carry, acc = tile(kn, vn, jnp.zeros((m, 1), F32), jnp.zeros((m, KV_WIDTH), F32), valid)
        carry_ref[...] = carry
        o_ref[0] = acc

    carry = carry_ref[...]
    acc = o_ref[0]
    for p in reversed(range(npg)):
        carry, acc = tile(k_refs[p][0], v_refs[p][0], carry, acc, None)
    carry_ref[...] = carry
    o_ref[0] = acc


def attn_sample(page_flat, q_pad, bias_rep, k_new, v_new, cache_k, cache_v, umat, n_pages):
    b, m, _ = q_pad.shape
    t_new = k_new.shape[1]
    npg = PAGES_PER_STEP
    nsteps = n_pages // npg

    def page_map(p):
        return lambda bb, s, pt: (pt[bb * n_pages + (nsteps - 1 - s) * npg + p], 0, 0)

    page_specs = [pl.BlockSpec((1, PAGE_SIZE, KV_WIDTH), page_map(p)) for p in range(npg)]
    grid_spec = pltpu.PrefetchScalarGridSpec(
        num_scalar_prefetch=1,
        grid=(b, nsteps),
        in_specs=[pl.BlockSpec((1, m, KV_WIDTH), lambda bb, s, pt: (bb, 0, 0)),
                  pl.BlockSpec((m, PAGE_SIZE), lambda bb, s, pt: (0, 0)),
                  pl.BlockSpec((1, t_new, KV_WIDTH), lambda bb, s, pt: (bb, 0, 0)),
                  pl.BlockSpec((1, t_new, KV_WIDTH), lambda bb, s, pt: (bb, 0, 0))]
                 + page_specs + page_specs
                 + [pl.BlockSpec((PAGE_SIZE, PAGE_SIZE), lambda bb, s, pt: (0, 0))],
        out_specs=pl.BlockSpec((1, m, KV_WIDTH), lambda bb, s, pt: (bb, 0, 0)),
        scratch_shapes=[pltpu.VMEM((m, 1), F32)],
    )
    return pl.pallas_call(
        functools.partial(_attn_sample_kernel, t_new=t_new),
        grid_spec=grid_spec,
        out_shape=jax.ShapeDtypeStruct((b, m, KV_WIDTH), F32),
        compiler_params=_cparams(("parallel", "arbitrary")),
        name="attn_sample",
    )(page_flat, q_pad, bias_rep, k_new, v_new, *([cache_k] * npg), *([cache_v] * npg), umat)


def _ssd_kernel(*refs, t_valid, has_state):
    it = iter(refs)
    xbc_ref, z_ref, dt_ref, cw_ref, cb_ref, dtb_ref, alog_ref, dsk_ref, nw_ref, tri_ref = (
        [next(it) for _ in range(10)])
    if has_state:
        cs_ref = next(it)
        h0_ref = next(it)
    y_ref, hs_ref, cso_ref, ext_ref, act_ref, ysc_ref = [next(it) for _ in range(6)]
    c = pl.program_id(1)
    ln = SSD_CHUNK
    pad = ln - t_valid

    @pl.when(c == 0)
    def _():
        if has_state:
            ext_ref[0:8, :] = cs_ref[0]
            hs_ref[...] = h0_ref[...]
        else:
            ext_ref[0:8, :] = jnp.zeros((8, CONV_CH), F32)
            hs_ref[...] = jnp.zeros(hs_ref.shape, F32)
        if pad:
            ext_ref[8 + t_valid:, :] = jnp.zeros((pad, CONV_CH), F32)

    ext_ref[8:8 + t_valid, :] = xbc_ref[...]
    cso_ref[0] = ext_ref[t_valid:t_valid + 8, :]

    for cc in range(CONV_CH // 512):
        sl = slice(cc * 512, (cc + 1) * 512)
        acc = cb_ref[:, sl] + cw_ref[3:4, sl] * ext_ref[8:8 + ln, sl]
        for j in range(1, CONV_W):
            acc = acc + cw_ref[3 - j:4 - j, sl] * ext_ref[8 - j:8 - j + ln, sl]
        act_ref[:, sl] = _silu(acc)

    ext_ref[0:8, :] = ext_ref[ln:ln + 8, :]

    row = lax.broadcasted_iota(jnp.int32, (ln, ln), 0)
    col = lax.broadcasted_iota(jnp.int32, (ln, ln), 1)

    dt_raw = dt_ref[...]
    if pad:
        dt_raw = jnp.concatenate([dt_raw, jnp.zeros((pad, DT_PAD), F32)], axis=0)
    xdt = dt_raw + dtb_ref[...]
    dtv = jnp.maximum(xdt, 0.0) + jnp.log1p(jnp.exp(-jnp.abs(xdt)))
    if pad:
        dtv = jnp.where(row < t_valid, dtv, 0.0)
    a_head = jnp.where(col[0:1, :] < SSD_HEADS, -jnp.exp(alog_ref[...]), 0.0)
    a = dtv * a_head

    tri = tri_ref[...]
    a1 = a.astype(BF16)
    r1 = a - a1.astype(F32)
    a2 = r1.astype(BF16)
    a3 = (r1 - a2.astype(F32)).astype(BF16)
    a_cum = (jnp.dot(tri, a1, preferred_element_type=F32) + jnp.dot(tri, a2, preferred_element_type=F32)
             + jnp.dot(tri, a3, preferred_element_type=F32))
    a_cum_t = a_cum.T
    dt_t = dtv.T
    last = jnp.broadcast_to(a_cum_t[:, ln - 1:ln], (ln, ln))
    chunk_decay = jnp.exp(last)
    upd_scale = jnp.exp(last - a_cum_t) * dt_t
    lower = row >= col
    lane_lo = col < SSD_HEAD_DIM
    sub_lo = row < SSD_HEAD_DIM

    for g in range(N_GROUPS):
        b_g = act_ref[:, D_INNER + g * D_STATE:D_INNER + (g + 1) * D_STATE]
        c_g = act_ref[:, D_INNER + N_GROUPS * D_STATE + g * D_STATE:D_INNER + N_GROUPS * D_STATE + (g + 1) * D_STATE]
        b_bf = b_g.astype(BF16)
        cb = lax.dot_general(c_g.astype(BF16), b_bf, (((1,), (1,)), ((), ())), preferred_element_type=F32)
        for pp in range(HEADS_PER_GROUP // 2):
            e0 = g * HEADS_PER_GROUP + 2 * pp
            lanes = slice(e0 * SSD_HEAD_DIM, (e0 + 2) * SSD_HEAD_DIM)
            xs_pair = act_ref[:, lanes]
            xs_bf = xs_pair.astype(BF16)
            h_pair = hs_ref[0, e0:e0 + 2].reshape(2 * SSD_HEAD_DIM, D_STATE)
            h_bf = h_pair.astype(BF16)
            ys = []
            for e in (e0, e0 + 1):
                colb = jnp.broadcast_to(a_cum[:, e:e + 1], (ln, ln))
                seg = colb - a_cum_t[e:e + 1, :]
                lmat = jnp.exp(jnp.where(lower, seg, -jnp.inf))
                m_e = (cb * lmat * dt_t[e:e + 1, :]).astype(BF16)
                c_e = (c_g * jnp.exp(colb)).astype(BF16)
                y_e = jnp.dot(m_e, xs_bf, preferred_element_type=F32)
                y_e = y_e + lax.dot_general(c_e, h_bf, (((1,), (1,)), ((), ())), preferred_element_type=F32)
                ys.append(y_e)
            ysc_ref[:, lanes] = jnp.where(lane_lo, ys[0], ys[1]) + dsk_ref[:, lanes] * xs_pair
            scale = jnp.where(sub_lo, upd_scale[e0:e0 + 1, :], upd_scale[e0 + 1:e0 + 2, :])
            upd = jnp.dot((xs_pair.T * scale).astype(BF16), b_bf, preferred_element_type=F32)
            decay = jnp.where(sub_lo, chunk_decay[e0:e0 + 1, :], chunk_decay[e0 + 1:e0 + 2, :])
            hs_ref[0, e0:e0 + 2] = (h_pair * decay + upd).reshape(2, SSD_HEAD_DIM, D_STATE)

    zz = z_ref[...]
    if pad:
        zz = jnp.concatenate([zz, jnp.zeros((pad, D_INNER), F32)], axis=0)
    gw = D_INNER // N_GROUPS
    for g in range(N_GROUPS):
        sl = slice(g * gw, (g + 1) * gw)
        blk = ysc_ref[:, sl] * _silu(zz[:, sl])
        ms = jnp.mean(blk * blk, axis=-1, keepdims=True)
        out = blk * lax.rsqrt(ms + EPS) * nw_ref[:, sl]
        y_ref[:, sl] = out[0:t_valid].astype(y_ref.dtype)


def ssd_mixer(proj, nb, nchunks, t_valid, conv_w, conv_b, dtb, alog, dsk, nw, tri, conv_state8=None, h0=None):
    has_state = h0 is not None
    n = proj.shape[0]
    row_map = lambda width_idx: (lambda bb, c: (bb * nchunks + c, width_idx))
    const = lambda bb, c: (0, 0)
    in_specs = [pl.BlockSpec((t_valid, CONV_CH), row_map(COL_XBC // CONV_CH)),
                pl.BlockSpec((t_valid, D_INNER), row_map(COL_Z // D_INNER)),
                pl.BlockSpec((t_valid, DT_PAD), row_map(COL_DT // DT_PAD)),
                pl.BlockSpec((CONV_W, CONV_CH), const),
                pl.BlockSpec((1, CONV_CH), const),
                pl.BlockSpec((1, DT_PAD), const),
                pl.BlockSpec((1, DT_PAD), const),
                pl.BlockSpec((1, D_INNER), const),
                pl.BlockSpec((1, D_INNER), const),
                pl.BlockSpec((SSD_CHUNK, SSD_CHUNK), const)]
    args = [proj, proj, proj, conv_w, conv_b, dtb, alog, dsk, nw, tri]
    if has_state:
        in_specs += [pl.BlockSpec((1, 8, CONV_CH), lambda bb, c: (bb, 0, 0)),
                     pl.BlockSpec((1, SSD_HEADS, SSD_HEAD_DIM, D_STATE), lambda bb, c: (bb, 0, 0, 0))]
        args += [conv_state8, h0]
    y_dtype = BF16 if t_valid % 16 == 0 else F32
    return pl.pallas_call(
        functools.partial(_ssd_kernel, t_valid=t_valid, has_state=has_state),
        grid=(nb, nchunks),
        in_specs=in_specs,
        out_specs=[pl.BlockSpec((t_valid, D_INNER), row_map(0)),
                   pl.BlockSpec((1, SSD_HEADS, SSD_HEAD_DIM, D_STATE), lambda bb, c: (bb, 0, 0, 0)),
                   pl.BlockSpec((1, 8, CONV_CH), lambda bb, c: (bb, 0, 0))],
        out_shape=[jax.ShapeDtypeStruct((n, D_INNER), y_dtype),
                   jax.ShapeDtypeStruct((nb, SSD_HEADS, SSD_HEAD_DIM, D_STATE), F32),
                   jax.ShapeDtypeStruct((nb, 8, CONV_CH), F32)],
        scratch_shapes=[pltpu.VMEM((SSD_CHUNK + 8, CONV_CH), F32),
                        pltpu.VMEM((SSD_CHUNK, CONV_CH), F32),
                        pltpu.VMEM((SSD_CHUNK, D_INNER), F32)],
        compiler_params=_cparams(("parallel", "arbitrary")),
        name="ssd_mixer",
    )(*args)


def _merge_kernel(attn_ref, y_ref, ga_ref, gb_ref, x_ref, wpa_ref, wpb_ref, wo_ref, o_ref):
    pa = jnp.dot(attn_ref[...].astype(BF16), wpa_ref[...], preferred_element_type=F32)
    pb = jnp.dot(y_ref[...].astype(BF16), wpb_ref[...], preferred_element_type=F32)
    sa = 1.0 / (1.0 + jnp.exp(-ga_ref[...]))
    sb = 1.0 / (1.0 + jnp.exp(-gb_ref[...]))
    merged = (sa * pa + sb * pb).astype(BF16)
    o_ref[...] = x_ref[...] + jnp.dot(merged, wo_ref[...], preferred_element_type=F32)


def merge_out(attn, y, proj, x2d, w_pa, w_pb, w_out, tm=512):
    n = x2d.shape[0]
    tm = min(tm, n)
    const = lambda i: (0, 0)
    return pl.pallas_call(
        _merge_kernel,
        grid=(n // tm,),
        in_specs=[pl.BlockSpec((tm, ATTN_WIDTH), lambda i: (i, 0)),
                  pl.BlockSpec((tm, D_INNER), lambda i: (i, 0)),
                  pl.BlockSpec((tm, D_MODEL), lambda i: (i, COL_GA // D_MODEL)),
                  pl.BlockSpec((tm, D_MODEL), lambda i: (i, COL_GB // D_MODEL)),
                  pl.BlockSpec((tm, D_MODEL), lambda i: (i, 0)),
                  pl.BlockSpec((ATTN_WIDTH, D_MODEL), const),
                  pl.BlockSpec((D_INNER, D_MODEL), const),
                  pl.BlockSpec((D_MODEL, D_MODEL), const)],
        out_specs=pl.BlockSpec((tm, D_MODEL), lambda i: (i, 0)),
        out_shape=jax.ShapeDtypeStruct((n, D_MODEL), F32),
        compiler_params=_cparams(("parallel",)),
        name="merge_out",
    )(attn, y, proj, proj, x2d, w_pa, w_pb, w_out)


def _ffn_kernel(x_ref, nw_ref, wg_ref, wu_ref, wd_ref, o_ref, h_ref):
    j = pl.program_id(1)

    @pl.when(j == 0)
    def _():
        x = x_ref[...]
        ms = jnp.mean(x * x, axis=-1, keepdims=True)
        h_ref[...] = (x * lax.rsqrt(ms + EPS) * nw_ref[...]).astype(BF16)
        o_ref[...] = x

    h = h_ref[...]
    gate = jnp.dot(h, wg_ref[...], preferred_element_type=F32)
    up = jnp.dot(h, wu_ref[...], preferred_element_type=F32)
    act = (_silu(gate) * up).astype(BF16)
    o_ref[...] += jnp.dot(act, wd_ref[...], preferred_element_type=F32)


def ffn(x2d, norm_w, w_gate, w_up, w_down, tm=512, tf=1408):
    n = x2d.shape[0]
    tm = min(tm, n)
    return pl.pallas_call(
        _ffn_kernel,
        grid=(n // tm, D_FF // tf),
        in_specs=[pl.BlockSpec((tm, D_MODEL), lambda i, j: (i, 0)),
                  pl.BlockSpec((1, D_MODEL), lambda i, j: (0, 0)),
                  pl.BlockSpec((D_MODEL, tf), lambda i, j: (0, j)),
                  pl.BlockSpec((D_MODEL, tf), lambda i, j: (0, j)),
                  pl.BlockSpec((tf, D_MODEL), lambda i, j: (j, 0))],
        out_specs=pl.BlockSpec((tm, D_MODEL), lambda i, j: (i, 0)),
        out_shape=jax.ShapeDtypeStruct((n, D_MODEL), F32),
        scratch_shapes=[pltpu.VMEM((tm, D_MODEL), BF16)],
        compiler_params=_cparams(("parallel", "arbitrary")),
        name="ffn",
    )(x2d, norm_w, w_gate, w_up, w_down)


def _strict_upper(n):
    r = np.arange(n)
    return jnp.asarray((r[:, None] > r[None, :]).astype(np.float32), BF16)


def _lower_incl(n):
    r = np.arange(n)
    return jnp.asarray((r[None, :] <= r[:, None]).astype(np.float32), BF16)


def _head_blockdiag(n, h):
    r = np.arange(n) // h
    return jnp.asarray((r[:, None] == r[None, :]).astype(np.float32), BF16)


def kernel(x_prompt, x_sample, cache_k, cache_v, page_table, state_ssm, state_conv, norm1_w, w_in, q_norm_w,
           k_norm_w, sb_bias, conv_w, conv_b, dt_bias, a_log, d_skip, ssd_norm_w, w_pa, w_pb, w_out, norm2_w,
           w_gate, w_up, w_down):
    bp, tp, _ = x_prompt.shape
    bd, td, _ = x_sample.shape
    n_pages = page_table.shape[1]

    wi = w_in[0]
    o = np.cumsum([0, ATTN_WIDTH, KV_WIDTH, KV_WIDTH, D_INNER, CONV_CH, SSD_HEADS, D_MODEL, D_MODEL])
    sec = [wi[:, o[i]:o[i + 1]] for i in range(8)]
    w_perm = jnp.concatenate([sec[3], sec[0], sec[4], sec[6], sec[7], sec[1], sec[2], sec[5],
                              jnp.zeros((D_MODEL, DT_PAD - SSD_HEADS), F32)], axis=1).astype(BF16)
    n1w = norm1_w[0][None, :]
    n2w = norm2_w[0][None, :]
    qw_t = jnp.tile(q_norm_w[0], 256 // HEAD_DIM)[None, :]
    kw_t = jnp.tile(k_norm_w[0], 256 // HEAD_DIM)[None, :]
    gmat = _head_blockdiag(256, HEAD_DIM)
    bias = sb_bias[0]
    cw = conv_w[0]
    cb = conv_b[0][None, :]
    dtb = jnp.pad(dt_bias[0], (0, DT_PAD - SSD_HEADS))[None, :]
    alog = jnp.pad(a_log[0], (0, DT_PAD - SSD_HEADS))[None, :]
    dsk = jnp.repeat(d_skip[0], SSD_HEAD_DIM)[None, :]
    snw = ssd_norm_w[0][None, :]
    tri = _lower_incl(SSD_CHUNK)
    wpa, wpb, wo = w_pa[0].astype(BF16), w_pb[0].astype(BF16), w_out[0].astype(BF16)
    wg, wu, wd = w_gate[0].astype(BF16), w_up[0].astype(BF16), w_down[0].astype(BF16)

    xp2 = x_prompt.reshape(bp * tp, D_MODEL)
    proj_p = in_proj(xp2, n1w, w_perm)
    q_p, k_p, v_p = qkv_post(proj_p, qw_t, kw_t, gmat)
    nbk = tp // ATT_TQ
    q_att = q_p.reshape(bp, tp, N_Q_HEADS, HEAD_DIM).transpose(0, 2, 1, 3)
    k5 = k_p.astype(BF16).reshape(bp, nbk, ATT_TQ, N_KV_HEADS, HEAD_DIM)
    v5 = v_p.astype(BF16).reshape(bp, nbk, ATT_TQ, N_KV_HEADS, HEAD_DIM)
    kt_att = k5.transpose(0, 3, 1, 4, 2)
    v_att = v5.transpose(0, 3, 1, 2, 4)
    attn_p = attn_prompt(q_att, kt_att, v_att, bias, _strict_upper(ATT_TQ)).reshape(bp * tp, ATTN_WIDTH)
    y_p, ssm_p, conv_p8 = ssd_mixer(proj_p, bp, tp // SSD_CHUNK, SSD_CHUNK, cw, cb, dtb, alog, dsk, snw, tri)
    x1_p = merge_out(attn_p, y_p, proj_p, xp2, wpa, wpb, wo)
    out_p = ffn(x1_p, n2w, wg, wu, wd).reshape(bp, tp, D_MODEL)

    xs2 = x_sample.reshape(bd * td, D_MODEL)
    proj_s = in_proj(xs2, n1w, w_perm)
    q_s, k_s, v_s = qkv_post(proj_s, qw_t, kw_t, gmat)
    m = N_Q_HEADS * td
    q_t = q_s.reshape(bd, td, N_Q_HEADS, HEAD_DIM).transpose(0, 2, 1, 3)
    onehot = jnp.asarray(np.arange(N_Q_HEADS)[:, None] // Q_PER_KV == np.arange(N_KV_HEADS)[None, :], BF16)
    q_pad = (q_t[:, :, :, None, :] * onehot[None, :, None, :, None]).reshape(bd, m, KV_WIDTH)
    bias_rep = jnp.broadcast_to(jnp.repeat(bias, td)[:, None], (m, PAGE_SIZE))
    ck = cache_k[0].reshape(-1, PAGE_SIZE, KV_WIDTH)
    cv = cache_v[0].reshape(-1, PAGE_SIZE, KV_WIDTH)
    o_s = attn_sample(page_table.reshape(-1), q_pad, bias_rep, k_s.reshape(bd, td, KV_WIDTH),
                      v_s.reshape(bd, td, KV_WIDTH), ck, cv, _strict_upper(PAGE_SIZE), n_pages)
    o_s = o_s.reshape(bd, N_KV_HEADS, Q_PER_KV, td, N_KV_HEADS, HEAD_DIM)
    attn_s = jnp.einsum('bgjtgd->btgjd', o_s).reshape(bd * td, ATTN_WIDTH).astype(BF16)
    cs8 = jnp.pad(state_conv[0], ((0, 0), (8 - (CONV_W - 1), 0), (0, 0)))
    y_s, ssm_s, conv_s8 = ssd_mixer(proj_s, bd, 1, td, cw, cb, dtb, alog, dsk, snw, tri, cs8, state_ssm[0])
    x1_s = merge_out(attn_s, y_s, proj_s, xs2, wpa, wpb, wo)
    out_s = ffn(x1_s, n2w, wg, wu, wd).reshape(bd, td, D_MODEL)

    kvp = (1, bp, tp, N_KV_HEADS, HEAD_DIM)
    kvs = (1, bd, td, N_KV_HEADS, HEAD_DIM)
    return (out_p, out_s, k_p.reshape(kvp), v_p.reshape(kvp), ssm_p[None], conv_p8[None, :, 8 - (CONV_W - 1):, :],
            k_s.reshape(kvs), v_s.reshape(kvs), ssm_s[None], conv_s8[None, :, 8 - (CONV_W - 1):, :])
```

```python
import functools

import jax
import jax.numpy as jnp
import numpy as np
from jax import lax
from jax.experimental import pallas as pl
from jax.experimental.pallas import tpu as pltpu

F32 = jnp.float32
BF16 = jnp.bfloat16

D_MODEL = 1024
HEAD_DIM = 64
N_Q_HEADS = 16
N_KV_HEADS = 4
Q_PER_KV = 4
ATTN_WIDTH = 1024
KV_WIDTH = 256
D_INNER = 2048
SSD_HEAD_DIM = 64
SSD_HEADS = 32
N_GROUPS = 4
HEADS_PER_GROUP = 8
D_STATE = 128
CONV_W = 4
CONV_CH = 3072
SSD_CHUNK = 128
D_FF = 2816
EPS = 1e-6
PAGE_SIZE = 128

COL_Z, COL_Q, COL_XBC, COL_GA, COL_GB, COL_K, COL_V, COL_DT = 0, 2048, 3072, 6144, 7168, 8192, 8448, 8704
DT_PAD = 128
PROJ_W = COL_DT + DT_PAD

VMEM_LIMIT = 56 * 1024 * 1024
ATT_TQ = 256
PAGES_PER_STEP = 16


def _cparams(sem):
    return pltpu.CompilerParams(dimension_semantics=sem, vmem_limit_bytes=VMEM_LIMIT)


def _split2(x):
    hi = x.astype(BF16)
    lo = (x - hi.astype(F32)).astype(BF16)
    return hi, lo


def _log1m_beta(zn):
    neg_abs = lax.bitcast_convert_type(lax.bitcast_convert_type(zn, jnp.uint32) | jnp.uint32(0x80000000), F32)
    return jnp.minimum(zn, 0.0) - jnp.log(1.0 + jnp.exp(neg_abs))


def _silu(x):
    return x * (1.0 / (1.0 + jnp.exp(-x)))


def _inproj_kernel(x_ref, nw_ref, w_ref, o_ref):
    x = x_ref[...]
    ms = jnp.mean(x * x, axis=-1, keepdims=True)
    xn = (x * lax.rsqrt(ms + EPS) * nw_ref[...]).astype(BF16)
    o_ref[...] = jnp.dot(xn, w_ref[...], preferred_element_type=F32)


def in_proj(x2d, norm_w, w_perm, tm=512, tn=2944):
    n, d = x2d.shape
    tm = min(tm, n)
    w = w_perm.shape[1]
    return pl.pallas_call(
        _inproj_kernel,
        grid=(w // tn, n // tm),
        in_specs=[pl.BlockSpec((tm, d), lambda j, i: (i, 0)),
                  pl.BlockSpec((1, d), lambda j, i: (0, 0)),
                  pl.BlockSpec((d, tn), lambda j, i: (0, j))],
        out_specs=pl.BlockSpec((tm, tn), lambda j, i: (i, j)),
        out_shape=jax.ShapeDtypeStruct((n, w), F32),
        compiler_params=_cparams(("parallel", "parallel")),
        name="in_proj",
    )(x2d, norm_w, w_perm)


def _qkv_kernel(q_ref, k_ref, v_ref, qw_ref, kw_ref, g_ref, qo_ref, ko_ref, vo_ref):
    gmat = g_ref[...]

    def headnorm(x, w):
        hi, lo = _split2(x * x)
        ss = jnp.dot(hi, gmat, preferred_element_type=F32) + jnp.dot(lo, gmat, preferred_element_type=F32)
        return x * lax.rsqrt(ss * (1.0 / HEAD_DIM) + EPS) * w

    qw = qw_ref[...]
    for c in range(ATTN_WIDTH // 256):
        sl = slice(c * 256, (c + 1) * 256)
        qo_ref[:, sl] = (headnorm(q_ref[:, sl], qw) * -(HEAD_DIM ** -0.5)).astype(BF16)
    ko_ref[...] = headnorm(k_ref[...], kw_ref[...])
    vo_ref[...] = v_ref[...]


def qkv_post(proj, qw_t, kw_t, gmat, tm=512):
    n = proj.shape[0]
    tm = min(tm, n)
    return pl.pallas_call(
        _qkv_kernel,
        grid=(n // tm,),
        in_specs=[pl.BlockSpec((tm, ATTN_WIDTH), lambda i: (i, COL_Q // ATTN_WIDTH)),
                  pl.BlockSpec((tm, KV_WIDTH), lambda i: (i, COL_K // KV_WIDTH)),
                  pl.BlockSpec((tm, KV_WIDTH), lambda i: (i, COL_V // KV_WIDTH)),
                  pl.BlockSpec((1, 256), lambda i: (0, 0)),
                  pl.BlockSpec((1, 256), lambda i: (0, 0)),
                  pl.BlockSpec((256, 256), lambda i: (0, 0))],
        out_specs=[pl.BlockSpec((tm, ATTN_WIDTH), lambda i: (i, 0)),
                   pl.BlockSpec((tm, KV_WIDTH), lambda i: (i, 0)),
                   pl.BlockSpec((tm, KV_WIDTH), lambda i: (i, 0))],
        out_shape=[jax.ShapeDtypeStruct((n, ATTN_WIDTH), BF16),
                   jax.ShapeDtypeStruct((n, KV_WIDTH), F32),
                   jax.ShapeDtypeStruct((n, KV_WIDTH), F32)],
        compiler_params=_cparams(("parallel",)),
        name="qkv_post",
    )(proj, proj, proj, qw_t, kw_t, gmat)


def _sb_logs(zn, valid):
    l1m = _log1m_beta(zn)
    if valid is not None:
        l1m = jnp.where(valid, l1m, 0.0)
    return jnp.concatenate(_split2(l1m), axis=1), jnp.sum(l1m, axis=-1, keepdims=True)


def _sb_weights(zn, suffix_incl, valid):
    w = jnp.exp(suffix_incl - zn)
    if valid is not None:
        w = jnp.where(valid, w, 0.0)
    return w.astype(BF16)


def _attn_prompt_kernel(q_ref, b3_ref, kt_ref, v_ref, u_ref, o_ref, zn_a, zn_b, hl_a, hl_b, w_a, w_b, acc_ref):
    i = pl.program_id(2)
    tq = q_ref.shape[1]
    u2 = u_ref[...]
    qblk = q_ref[0]
    heads = range(Q_PER_KV)
    qa = [jnp.concatenate([qblk[:, j * HEAD_DIM:(j + 1) * HEAD_DIM],
                           jnp.broadcast_to(b3_ref[0, j:j + 1, :], (tq, HEAD_DIM)).astype(BF16)], axis=1)
          for j in heads]

    zn_ref, hl_ref, w_ref = (zn_a, zn_b), (hl_a, hl_b), (w_a, w_b)

    def stage1(jj, s, cum, valid):
        kt = kt_ref[0, 0, jj]
        new = []
        for j in heads:
            zn = jnp.dot(qa[j], kt, preferred_element_type=F32)
            hl, rs = _sb_logs(zn, valid)
            zn_ref[s][j] = zn
            hl_ref[s][j] = hl
            new.append(cum[j] + rs)
        return cum, new

    def stage2(s_in, s_out, valid):
        for j in heads:
            sfx = jnp.dot(hl_ref[s_in][j], u2, preferred_element_type=F32)
            w_ref[s_out][j] = _sb_weights(zn_ref[s_in][j], sfx, valid)

    def stage3(jj, s_in, c, live):
        v = v_ref[0, 0, jj]
        for j in heads:
            scale = jnp.where(live, jnp.exp(c[j]), 0.0)
            acc_ref[j] += scale * jnp.dot(w_ref[s_in][j], v, preferred_element_type=F32)

    row = lax.broadcasted_iota(jnp.int32, (tq, tq), 0)
    col = lax.broadcasted_iota(jnp.int32, (tq, tq), 1)
    diag = col < row
    acc_ref[...] = jnp.zeros(acc_ref.shape, F32)
    cw, cum = stage1(i, 0, [jnp.zeros((tq, 1), F32)] * Q_PER_KV, diag)
    stage2(0, 1, diag)
    c, cum = stage1(jnp.maximum(i - 1, 0), 1, cum, None)

    def trip(t, s, st):
        c, cw, cum = st
        c_n, cum = stage1(jnp.maximum(i - t - 2, 0), s, cum, None)
        stage2(1 - s, s, None)
        stage3(jnp.maximum(i - t, 0), 1 - s, cw, t <= i)
        return c_n, c, cum

    lax.fori_loop(0, (i + 2) // 2, lambda u, st: trip(2 * u + 1, 1, trip(2 * u, 0, st)), (c, cw, cum))
    o_ref[0] = jnp.concatenate([acc_ref[j] for j in heads], axis=-1).astype(BF16)


def attn_prompt(q_btd, nb3, kt_att, v_att, u2):
    b, t, _ = q_btd.shape
    tq = u2.shape[1]
    nb = t // tq
    gw = Q_PER_KV * HEAD_DIM
    return pl.pallas_call(
        _attn_prompt_kernel,
        grid=(b, N_KV_HEADS, nb),
        in_specs=[pl.BlockSpec((1, tq, gw), lambda bb, g, i: (bb, i, g)),
                  pl.BlockSpec((1, Q_PER_KV, HEAD_DIM), lambda bb, g, i: (g, 0, 0)),
                  pl.BlockSpec((1, 1, nb, 2 * HEAD_DIM, tq), lambda bb, g, i: (bb, g, 0, 0, 0)),
                  pl.BlockSpec((1, 1, nb, tq, HEAD_DIM), lambda bb, g, i: (bb, g, 0, 0, 0)),
                  pl.BlockSpec((2 * tq, tq), lambda bb, g, i: (0, 0))],
        out_specs=pl.BlockSpec((1, tq, gw), lambda bb, g, i: (bb, i, g)),
        out_shape=jax.ShapeDtypeStruct((b, t, ATTN_WIDTH), BF16),
        scratch_shapes=[pltpu.VMEM((Q_PER_KV, tq, tq), F32)] * 2
                       + [pltpu.VMEM((Q_PER_KV, tq, 2 * tq), BF16)] * 2
                       + [pltpu.VMEM((Q_PER_KV, tq, tq), BF16)] * 2
                       + [pltpu.VMEM((Q_PER_KV, tq, HEAD_DIM), F32)],
        compiler_params=_cparams(("parallel", "parallel", "parallel")),
        name="attn_prompt",
    )(q_btd, nb3, kt_att, v_att, u2)


def _attn_sample_kernel(pt_ref, q_ref, nbias_ref, knt_ref, vnt_ref, *rest, t_new, npg):
    k_refs = rest[:npg]
    v_refs = rest[npg:2 * npg]
    u_ref, un_ref, o_ref, carry_ref = rest[2 * npg:]
    s = pl.program_id(1)
    q = q_ref[0]
    nbias = nbias_ref[...]
    m = q.shape[0]
    pg = PAGE_SIZE

    nt = (((1,), (1,)), ((), ()))

    @pl.when(s == 0)
    def _():
        zn = jnp.dot(q, knt_ref[0].astype(BF16), preferred_element_type=F32) + nbias[:, 0:pg]
        row = lax.broadcasted_iota(jnp.int32, (m, pg), 0)
        col = lax.broadcasted_iota(jnp.int32, (m, pg), 1)
        valid = col < (row % t_new)
        hl, rs = _sb_logs(zn, valid)
        w = _sb_weights(zn, jnp.dot(hl, un_ref[...], preferred_element_type=F32), valid)
        carry_ref[...] = rs
        o_ref[0] = lax.dot_general(w, vnt_ref[0].astype(BF16), nt, preferred_element_type=F32)

    pph = npg // 2
    nblk = pph // 2
    u2 = u_ref[...]

    def logits(h):
        kt = jnp.concatenate([k_refs[p][0].reshape(KV_WIDTH, pg).astype(BF16)
                              for p in range(h * pph, (h + 1) * pph)], axis=1)
        zc = jnp.dot(q, kt, preferred_element_type=F32)
        return jnp.concatenate([zc[:, b * 2 * pg:(b + 1) * 2 * pg] + nbias for b in range(nblk)], axis=0)

    def logs(zn, run):
        hl, rs = _sb_logs(zn, None)
        carries = [None] * nblk
        for b in reversed(range(nblk)):
            carries[b] = run
            run = run + rs[b * m:(b + 1) * m]
        return hl, jnp.concatenate(carries, axis=0), run

    def weighted_values(zn, suffix, h):
        w = _sb_weights(zn, suffix, None)
        w_cat = jnp.concatenate([w[b * m:(b + 1) * m] for b in range(nblk)], axis=1)
        vt = jnp.concatenate([v_refs[p][0].reshape(KV_WIDTH, pg).astype(BF16)
                              for p in range(h * pph, (h + 1) * pph)], axis=1)
        return lax.dot_general(w_cat, vt, nt, preferred_element_type=F32)

    zn_hi = logits(1)
    zn_lo = logits(0)
    hl_hi, c_hi, run = logs(zn_hi, carry_ref[...])
    sfx_hi = jnp.dot(hl_hi, u2, preferred_element_type=F32) + c_hi
    hl_lo, c_lo, run = logs(zn_lo, run)
    sfx_lo = jnp.dot(hl_lo, u2, preferred_element_type=F32) + c_lo
    carry_ref[...] = run
    o_ref[0] += weighted_values(zn_hi, sfx_hi, 1) + weighted_values(zn_lo, sfx_lo, 0)


def attn_sample(page_flat, q_pad, nbias_rep, knt, vnt, cache_kt, cache_vt, umat2, umat1, n_pages, t_new):
    b, m, _ = q_pad.shape
    npg = PAGES_PER_STEP
    nsteps = n_pages // npg

    def page_map(p):
        return lambda bb, s, pt: (pt[bb * n_pages + (nsteps - 1 - s) * npg + p], 0, 0, 0)

    page_specs = [pl.BlockSpec((1, N_KV_HEADS, HEAD_DIM, PAGE_SIZE), page_map(p)) for p in range(npg)]
    grid_spec = pltpu.PrefetchScalarGridSpec(
        num_scalar_prefetch=1,
        grid=(b, nsteps),
        in_specs=[pl.BlockSpec((1, m, KV_WIDTH), lambda bb, s, pt: (bb, 0, 0)),
                  pl.BlockSpec((m, 2 * PAGE_SIZE), lambda bb, s, pt: (0, 0)),
                  pl.BlockSpec((1, KV_WIDTH, PAGE_SIZE), lambda bb, s, pt: (bb, 0, 0)),
                  pl.BlockSpec((1, KV_WIDTH, PAGE_SIZE), lambda bb, s, pt: (bb, 0, 0))]
                 + page_specs + page_specs
                 + [pl.BlockSpec((4 * PAGE_SIZE, 2 * PAGE_SIZE), lambda bb, s, pt: (0, 0)),
                    pl.BlockSpec((2 * PAGE_SIZE, PAGE_SIZE), lambda bb, s, pt: (0, 0))],
        out_specs=pl.BlockSpec((1, m, KV_WIDTH), lambda bb, s, pt: (bb, 0, 0)),
        scratch_shapes=[pltpu.VMEM((m, 1), F32)],
    )
    return pl.pallas_call(
        functools.partial(_attn_sample_kernel, t_new=t_new, npg=npg),
        grid_spec=grid_spec,
        out_shape=jax.ShapeDtypeStruct((b, m, KV_WIDTH), F32),
        compiler_params=_cparams(("parallel", "arbitrary")),
        name="attn_sample",
    )(page_flat, q_pad, nbias_rep, knt, vnt, *([cache_kt] * npg), *([cache_vt] * npg), umat2, umat1)


def _ssd_kernel(*refs, t_valid, has_state):
    it = iter(refs)
    xbc_ref, z_ref, dt_ref, cw_ref, cb_ref, dtb_ref, alog_ref, dsk_ref, nw_ref, tri_ref = (
        [next(it) for _ in range(10)])
    if has_state:
        cs_ref = next(it)
        h0_ref = next(it)
    y_ref, hs_ref, cso_ref, ext_ref, act_ref, ysc_ref = [next(it) for _ in range(6)]
    c = pl.program_id(1)
    ln = SSD_CHUNK
    pad = ln - t_valid

    @pl.when(c == 0)
    def _():
        if has_state:
            ext_ref[0:8, :] = cs_ref[0]
            hs_ref[...] = h0_ref[...]
        else:
            ext_ref[0:8, :] = jnp.zeros((8, CONV_CH), F32)
            hs_ref[...] = jnp.zeros(hs_ref.shape, F32)
        if pad:
            ext_ref[8 + t_valid:, :] = jnp.zeros((pad, CONV_CH), F32)

    ext_ref[8:8 + t_valid, :] = xbc_ref[...]
    cso_ref[0] = ext_ref[t_valid:t_valid + 8, :]

    for cc in range(CONV_CH // 512):
        sl = slice(cc * 512, (cc + 1) * 512)
        acc = cb_ref[:, sl] + cw_ref[3:4, sl] * ext_ref[8:8 + ln, sl]
        for j in range(1, CONV_W):
            acc = acc + cw_ref[3 - j:4 - j, sl] * ext_ref[8 - j:8 - j + ln, sl]
        act_ref[:, sl] = _silu(acc)

    ext_ref[0:8, :] = ext_ref[ln:ln + 8, :]

    row = lax.broadcasted_iota(jnp.int32, (ln, ln), 0)
    col = lax.broadcasted_iota(jnp.int32, (ln, ln), 1)

    dt_raw = dt_ref[...]
    if pad:
        dt_raw = jnp.concatenate([dt_raw, jnp.zeros((pad, DT_PAD), F32)], axis=0)
    xdt = dt_raw + dtb_ref[...]
    dtv = jnp.maximum(xdt, 0.0) + jnp.log1p(jnp.exp(-jnp.abs(xdt)))
    if pad:
        dtv = jnp.where(row < t_valid, dtv, 0.0)
    a_head = jnp.where(col[0:1, :] < SSD_HEADS, -jnp.exp(alog_ref[...]), 0.0)
    a = dtv * a_head

    tri = tri_ref[...]
    a1 = a.astype(BF16)
    r1 = a - a1.astype(F32)
    a2 = r1.astype(BF16)
    a3 = (r1 - a2.astype(F32)).astype(BF16)
    a_cum = (jnp.dot(tri, a1, preferred_element_type=F32) + jnp.dot(tri, a2, preferred_element_type=F32)
             + jnp.dot(tri, a3, preferred_element_type=F32))
    a_cum_t = a_cum.T
    dt_t = dtv.T
    last = jnp.broadcast_to(a_cum_t[:, ln - 1:ln], (ln, ln))
    chunk_decay = jnp.exp(last)
    upd_scale = jnp.exp(last - a_cum_t) * dt_t
    lower = row >= col
    lane_lo = col < SSD_HEAD_DIM
    sub_lo = row < SSD_HEAD_DIM

    for g in range(N_GROUPS):
        b_g = act_ref[:, D_INNER + g * D_STATE:D_INNER + (g + 1) * D_STATE]
        c_g = act_ref[:, D_INNER + N_GROUPS * D_STATE + g * D_STATE:D_INNER + N_GROUPS * D_STATE + (g + 1) * D_STATE]
        b_bf = b_g.astype(BF16)
        cb = lax.dot_general(c_g.astype(BF16), b_bf, (((1,), (1,)), ((), ())), preferred_element_type=F32)
        for pp in range(HEADS_PER_GROUP // 2):
            e0 = g * HEADS_PER_GROUP + 2 * pp
            lanes = slice(e0 * SSD_HEAD_DIM, (e0 + 2) * SSD_HEAD_DIM)
            xs_pair = act_ref[:, lanes]
            xs_bf = xs_pair.astype(BF16)
            h_pair = hs_ref[0, e0:e0 + 2].reshape(2 * SSD_HEAD_DIM, D_STATE)
            h_bf = h_pair.astype(BF16)
            ys = []
            for e in (e0, e0 + 1):
                colb = jnp.broadcast_to(a_cum[:, e:e + 1], (ln, ln))
                seg = colb - a_cum_t[e:e + 1, :]
                lmat = jnp.exp(jnp.where(lower, seg, -jnp.inf))
                m_e = (cb * lmat * dt_t[e:e + 1, :]).astype(BF16)
                c_e = (c_g * jnp.exp(colb)).astype(BF16)
                y_e = jnp.dot(m_e, xs_bf, preferred_element_type=F32)
                y_e = y_e + lax.dot_general(c_e, h_bf, (((1,), (1,)), ((), ())), preferred_element_type=F32)
                ys.append(y_e)
            ysc_ref[:, lanes] = jnp.where(lane_lo, ys[0], ys[1]) + dsk_ref[:, lanes] * xs_pair
            scale = jnp.where(sub_lo, upd_scale[e0:e0 + 1, :], upd_scale[e0 + 1:e0 + 2, :])
            upd = jnp.dot((xs_pair.T * scale).astype(BF16), b_bf, preferred_element_type=F32)
            decay = jnp.where(sub_lo, chunk_decay[e0:e0 + 1, :], chunk_decay[e0 + 1:e0 + 2, :])
            hs_ref[0, e0:e0 + 2] = (h_pair * decay + upd).reshape(2, SSD_HEAD_DIM, D_STATE)

    zz = z_ref[...]
    if pad:
        zz = jnp.concatenate([zz, jnp.zeros((pad, D_INNER), F32)], axis=0)
    gw = D_INNER // N_GROUPS
    for g in range(N_GROUPS):
        sl = slice(g * gw, (g + 1) * gw)
        blk = ysc_ref[:, sl] * _silu(zz[:, sl])
        ms = jnp.mean(blk * blk, axis=-1, keepdims=True)
        out = blk * lax.rsqrt(ms + EPS) * nw_ref[:, sl]
        y_ref[:, sl] = out[0:t_valid].astype(y_ref.dtype)


def ssd_mixer(proj, nb, nchunks, t_valid, conv_w, conv_b, dtb, alog, dsk, nw, tri, conv_state8=None, h0=None):
    has_state = h0 is not None
    n = proj.shape[0]
    row_map = lambda width_idx: (lambda bb, c: (bb * nchunks + c, width_idx))
    const = lambda bb, c: (0, 0)
    in_specs = [pl.BlockSpec((t_valid, CONV_CH), row_map(COL_XBC // CONV_CH)),
                pl.BlockSpec((t_valid, D_INNER), row_map(COL_Z // D_INNER)),
                pl.BlockSpec((t_valid, DT_PAD), row_map(COL_DT // DT_PAD)),
                pl.BlockSpec((CONV_W, CONV_CH), const),
                pl.BlockSpec((1, CONV_CH), const),
                pl.BlockSpec((1, DT_PAD), const),
                pl.BlockSpec((1, DT_PAD), const),
                pl.BlockSpec((1, D_INNER), const),
                pl.BlockSpec((1, D_INNER), const),
                pl.BlockSpec((SSD_CHUNK, SSD_CHUNK), const)]
    args = [proj, proj, proj, conv_w, conv_b, dtb, alog, dsk, nw, tri]
    if has_state:
        in_specs += [pl.BlockSpec((1, 8, CONV_CH), lambda bb, c: (bb, 0, 0)),
                     pl.BlockSpec((1, SSD_HEADS, SSD_HEAD_DIM, D_STATE), lambda bb, c: (bb, 0, 0, 0))]
        args += [conv_state8, h0]
    y_dtype = BF16 if t_valid % 16 == 0 else F32
    return pl.pallas_call(
        functools.partial(_ssd_kernel, t_valid=t_valid, has_state=has_state),
        grid=(nb, nchunks),
        in_specs=in_specs,
        out_specs=[pl.BlockSpec((t_valid, D_INNER), row_map(0)),
                   pl.BlockSpec((1, SSD_HEADS, SSD_HEAD_DIM, D_STATE), lambda bb, c: (bb, 0, 0, 0)),
                   pl.BlockSpec((1, 8, CONV_CH), lambda bb, c: (bb, 0, 0))],
        out_shape=[jax.ShapeDtypeStruct((n, D_INNER), y_dtype),
                   jax.ShapeDtypeStruct((nb, SSD_HEADS, SSD_HEAD_DIM, D_STATE), F32),
                   jax.ShapeDtypeStruct((nb, 8, CONV_CH), F32)],
        scratch_shapes=[pltpu.VMEM((SSD_CHUNK + 8, CONV_CH), F32),
                        pltpu.VMEM((SSD_CHUNK, CONV_CH), F32),
                        pltpu.VMEM((SSD_CHUNK, D_INNER), F32)],
        compiler_params=_cparams(("parallel", "arbitrary")),
        name="ssd_mixer",
    )(*args)


def _merge_kernel(attn_ref, y_ref, ga_ref, gb_ref, x_ref, wpa_ref, wpb_ref, wo_ref, o_ref):
    pa = jnp.dot(attn_ref[...].astype(BF16), wpa_ref[...], preferred_element_type=F32)
    pb = jnp.dot(y_ref[...].astype(BF16), wpb_ref[...], preferred_element_type=F32)
    sa = 1.0 / (1.0 + jnp.exp(-ga_ref[...]))
    sb = 1.0 / (1.0 + jnp.exp(-gb_ref[...]))
    merged = (sa * pa + sb * pb).astype(BF16)
    o_ref[...] = x_ref[...] + jnp.dot(merged, wo_ref[...], preferred_element_type=F32)


def merge_out(attn, y, proj, x2d, w_pa, w_pb, w_out, tm=512):
    n = x2d.shape[0]
    tm = min(tm, n)
    const = lambda i: (0, 0)
    return pl.pallas_call(
        _merge_kernel,
        grid=(n // tm,),
        in_specs=[pl.BlockSpec((tm, ATTN_WIDTH), lambda i: (i, 0)),
                  pl.BlockSpec((tm, D_INNER), lambda i: (i, 0)),
                  pl.BlockSpec((tm, D_MODEL), lambda i: (i, COL_GA // D_MODEL)),
                  pl.BlockSpec((tm, D_MODEL), lambda i: (i, COL_GB // D_MODEL)),
                  pl.BlockSpec((tm, D_MODEL), lambda i: (i, 0)),
                  pl.BlockSpec((ATTN_WIDTH, D_MODEL), const),
                  pl.BlockSpec((D_INNER, D_MODEL), const),
                  pl.BlockSpec((D_MODEL, D_MODEL), const)],
        out_specs=pl.BlockSpec((tm, D_MODEL), lambda i: (i, 0)),
        out_shape=jax.ShapeDtypeStruct((n, D_MODEL), F32),
        compiler_params=_cparams(("parallel",)),
        name="merge_out",
    )(attn, y, proj, proj, x2d, w_pa, w_pb, w_out)


def _ffn_kernel(x_ref, nw_ref, wg_ref, wu_ref, wd_ref, o_ref, h_ref):
    j = pl.program_id(1)

    @pl.when(j == 0)
    def _():
        x = x_ref[...]
        ms = jnp.mean(x * x, axis=-1, keepdims=True)
        h_ref[...] = (x * lax.rsqrt(ms + EPS) * nw_ref[...]).astype(BF16)
        o_ref[...] = x

    h = h_ref[...]
    gate = jnp.dot(h, wg_ref[...], preferred_element_type=F32)
    up = jnp.dot(h, wu_ref[...], preferred_element_type=F32)
    act = (_silu(gate) * up).astype(BF16)
    o_ref[...] += jnp.dot(act, wd_ref[...], preferred_element_type=F32)


def ffn(x2d, norm_w, w_gate, w_up, w_down, tm=512, tf=1408):
    n = x2d.shape[0]
    tm = min(tm, n)
    return pl.pallas_call(
        _ffn_kernel,
        grid=(n // tm, D_FF // tf),
        in_specs=[pl.BlockSpec((tm, D_MODEL), lambda i, j: (i, 0)),
                  pl.BlockSpec((1, D_MODEL), lambda i, j: (0, 0)),
                  pl.BlockSpec((D_MODEL, tf), lambda i, j: (0, j)),
                  pl.BlockSpec((D_MODEL, tf), lambda i, j: (0, j)),
                  pl.BlockSpec((tf, D_MODEL), lambda i, j: (j, 0))],
        out_specs=pl.BlockSpec((tm, D_MODEL), lambda i, j: (i, 0)),
        out_shape=jax.ShapeDtypeStruct((n, D_MODEL), F32),
        scratch_shapes=[pltpu.VMEM((tm, D_MODEL), BF16)],
        compiler_params=_cparams(("parallel", "arbitrary")),
        name="ffn",
    )(x2d, norm_w, w_gate, w_up, w_down)


def _suffix_sum_matrix(n):
    r = np.arange(n)
    u = (r[:, None] >= r[None, :]).astype(np.float32)
    return jnp.asarray(np.concatenate([u, u], axis=0), BF16)


def _lower_incl(n):
    r = np.arange(n)
    return jnp.asarray((r[None, :] <= r[:, None]).astype(np.float32), BF16)


def _head_blockdiag(n, h):
    r = np.arange(n) // h
    return jnp.asarray((r[:, None] == r[None, :]).astype(np.float32), BF16)


def kernel(x_prompt, x_sample, cache_k, cache_v, page_table, state_ssm, state_conv, norm1_w, w_in, q_norm_w,
           k_norm_w, sb_bias, conv_w, conv_b, dt_bias, a_log, d_skip, ssd_norm_w, w_pa, w_pb, w_out, norm2_w,
           w_gate, w_up, w_down):
    bp, tp, _ = x_prompt.shape
    bd, td, _ = x_sample.shape
    n_pages = page_table.shape[1]

    wi = w_in[0]
    o = np.cumsum([0, ATTN_WIDTH, KV_WIDTH, KV_WIDTH, D_INNER, CONV_CH, SSD_HEADS, D_MODEL, D_MODEL])
    sec = [wi[:, o[i]:o[i + 1]] for i in range(8)]
    w_perm = jnp.concatenate([sec[3], sec[0], sec[4], sec[6], sec[7], sec[1], sec[2], sec[5],
                              jnp.zeros((D_MODEL, DT_PAD - SSD_HEADS), F32)], axis=1).astype(BF16)
    n1w = norm1_w[0][None, :]
    n2w = norm2_w[0][None, :]
    qw_t = jnp.tile(q_norm_w[0], 256 // HEAD_DIM)[None, :]
    kw_t = jnp.tile(k_norm_w[0], 256 // HEAD_DIM)[None, :]
    gmat = _head_blockdiag(256, HEAD_DIM)
    nbias = -sb_bias[0]
    cw = conv_w[0]
    cb = conv_b[0][None, :]
    dtb = jnp.pad(dt_bias[0], (0, DT_PAD - SSD_HEADS))[None, :]
    alog = jnp.pad(a_log[0], (0, DT_PAD - SSD_HEADS))[None, :]
    dsk = jnp.repeat(d_skip[0], SSD_HEAD_DIM)[None, :]
    snw = ssd_norm_w[0][None, :]
    tri = _lower_incl(SSD_CHUNK)
    wpa, wpb, wo = w_pa[0].astype(BF16), w_pb[0].astype(BF16), w_out[0].astype(BF16)
    wg, wu, wd = w_gate[0].astype(BF16), w_up[0].astype(BF16), w_down[0].astype(BF16)

    xp2 = x_prompt.reshape(bp * tp, D_MODEL)
    proj_p = in_proj(xp2, n1w, w_perm)
    q_p, k_p, v_p = qkv_post(proj_p, qw_t, kw_t, gmat)
    nbk = tp // ATT_TQ
    k5 = k_p.astype(BF16).reshape(bp, nbk, ATT_TQ, N_KV_HEADS, HEAD_DIM)
    v5 = v_p.astype(BF16).reshape(bp, nbk, ATT_TQ, N_KV_HEADS, HEAD_DIM)
    ones_rows = jnp.zeros((HEAD_DIM, ATT_TQ), BF16).at[0:3].set(1)
    kt_att = jnp.concatenate([k5.transpose(0, 3, 1, 4, 2),
                              jnp.broadcast_to(ones_rows, (bp, N_KV_HEADS, nbk, HEAD_DIM, ATT_TQ))], axis=3)
    v_att = v5.transpose(0, 3, 1, 2, 4)
    b1 = nbias.astype(BF16)
    b2 = (nbias - b1.astype(F32)).astype(BF16)
    b3 = (nbias - b1.astype(F32) - b2.astype(F32)).astype(BF16)
    nb3 = jnp.zeros((N_Q_HEADS, HEAD_DIM), F32).at[:, 0:3].set(jnp.stack([b1, b2, b3], axis=1).astype(F32))
    attn_p = attn_prompt(q_p.reshape(bp, tp, ATTN_WIDTH), nb3.reshape(N_KV_HEADS, Q_PER_KV, HEAD_DIM), kt_att, v_att,
                         _suffix_sum_matrix(ATT_TQ)).reshape(bp * tp, ATTN_WIDTH)
    y_p, ssm_p, conv_p8 = ssd_mixer(proj_p, bp, tp // SSD_CHUNK, SSD_CHUNK, cw, cb, dtb, alog, dsk, snw, tri)
    x1_p = merge_out(attn_p, y_p, proj_p, xp2, wpa, wpb, wo)
    out_p = ffn(x1_p, n2w, wg, wu, wd).reshape(bp, tp, D_MODEL)

    xs2 = x_sample.reshape(bd * td, D_MODEL)
    proj_s = in_proj(xs2, n1w, w_perm)
    q_s, k_s, v_s = qkv_post(proj_s, qw_t, kw_t, gmat)
    m = N_Q_HEADS * td
    q_t = q_s.reshape(bd, td, N_Q_HEADS, HEAD_DIM).transpose(0, 2, 1, 3)
    onehot = jnp.asarray(np.arange(N_Q_HEADS)[:, None] // Q_PER_KV == np.arange(N_KV_HEADS)[None, :], BF16)
    q_pad = (q_t[:, :, :, None, :] * onehot[None, :, None, :, None]).reshape(bd, m, KV_WIDTH)
    nbias_rep = jnp.broadcast_to(jnp.repeat(nbias, td)[:, None], (m, 2 * PAGE_SIZE))
    lane_pad = ((0, 0), (0, 0), (0, PAGE_SIZE - td))
    knt = jnp.pad(k_s.reshape(bd, td, KV_WIDTH).transpose(0, 2, 1), lane_pad)
    vnt = jnp.pad(v_s.reshape(bd, td, KV_WIDTH).transpose(0, 2, 1), lane_pad)
    ckt = cache_k[0].transpose(0, 2, 3, 1)
    cvt = cache_v[0].transpose(0, 2, 3, 1)
    o_s = attn_sample(page_table.reshape(-1), q_pad, nbias_rep, knt, vnt, ckt, cvt,
                      _suffix_sum_matrix(2 * PAGE_SIZE), _suffix_sum_matrix(PAGE_SIZE), n_pages, td)
    o_s = o_s.reshape(bd, N_KV_HEADS, Q_PER_KV, td, N_KV_HEADS, HEAD_DIM)
    attn_s = jnp.einsum('bgjtgd->btgjd', o_s).reshape(bd * td, ATTN_WIDTH).astype(BF16)
    cs8 = jnp.pad(state_conv[0], ((0, 0), (8 - (CONV_W - 1), 0), (0, 0)))
    y_s, ssm_s, conv_s8 = ssd_mixer(proj_s, bd, 1, td, cw, cb, dtb, alog, dsk, snw, tri, cs8, state_ssm[0])
    x1_s = merge_out(attn_s, y_s, proj_s, xs2, wpa, wpb, wo)
    out_s = ffn(x1_s, n2w, wg, wu, wd).reshape(bd, td, D_MODEL)

    kvp = (1, bp, tp, N_KV_HEADS, HEAD_DIM)
    kvs = (1, bd, td, N_KV_HEADS, HEAD_DIM)
    return (out_p, out_s, k_p.reshape(kvp), v_p.reshape(kvp), ssm_p[None], conv_p8[None, :, 8 - (CONV_W - 1):, :],
            k_s.reshape(kvs), v_s.reshape(kvs), ssm_s[None], conv_s8[None, :, 8 - (CONV_W - 1):, :])
```

```python
import functools

import jax
import jax.numpy as jnp
import numpy as np
from jax import lax
from jax.experimental import pallas as pl
from jax.experimental.pallas import tpu as pltpu

F32 = jnp.float32
BF16 = jnp.bfloat16

D_MODEL = 1024
HEAD_DIM = 64
N_Q_HEADS = 16
N_KV_HEADS = 4
Q_PER_KV = 4
ATTN_WIDTH = 1024
KV_WIDTH = 256
D_INNER = 2048
SSD_HEAD_DIM = 64
SSD_HEADS = 32
N_GROUPS = 4
HEADS_PER_GROUP = 8
D_STATE = 128
CONV_W = 4
CONV_CH = 3072
SSD_CHUNK = 128
D_FF = 2816
EPS = 1e-6
PAGE_SIZE = 128

COL_Z, COL_Q, COL_XBC, COL_GA, COL_GB, COL_K, COL_V, COL_DT = 0, 2048, 3072, 6144, 7168, 8192, 8448, 8704
DT_PAD = 128
PROJ_W = COL_DT + DT_PAD

VMEM_LIMIT = 56 * 1024 * 1024
ATT_TQ = 256
PAGES_PER_STEP = 16


def _cparams(sem):
    return pltpu.CompilerParams(dimension_semantics=sem, vmem_limit_bytes=VMEM_LIMIT)


def _split2(x):
    hi = x.astype(BF16)
    lo = (x - hi.astype(F32)).astype(BF16)
    return hi, lo


def _log1m_beta(zn):
    neg_abs = lax.bitcast_convert_type(lax.bitcast_convert_type(zn, jnp.uint32) | jnp.uint32(0x80000000), F32)
    return jnp.minimum(zn, 0.0) - jnp.log(1.0 + jnp.exp(neg_abs))


def _silu(x):
    return x * (1.0 / (1.0 + jnp.exp(-x)))


def _inproj_kernel(x_ref, nw_ref, w_ref, o_ref):
    x = x_ref[...]
    ms = jnp.mean(x * x, axis=-1, keepdims=True)
    xn = (x * lax.rsqrt(ms + EPS) * nw_ref[...]).astype(BF16)
    o_ref[...] = jnp.dot(xn, w_ref[...], preferred_element_type=F32)


def in_proj(x2d, norm_w, w_perm, tm=512, tn=2944):
    n, d = x2d.shape
    tm = min(tm, n)
    w = w_perm.shape[1]
    return pl.pallas_call(
        _inproj_kernel,
        grid=(w // tn, n // tm),
        in_specs=[pl.BlockSpec((tm, d), lambda j, i: (i, 0)),
                  pl.BlockSpec((1, d), lambda j, i: (0, 0)),
                  pl.BlockSpec((d, tn), lambda j, i: (0, j))],
        out_specs=pl.BlockSpec((tm, tn), lambda j, i: (i, j)),
        out_shape=jax.ShapeDtypeStruct((n, w), F32),
        compiler_params=_cparams(("parallel", "parallel")),
        name="in_proj",
    )(x2d, norm_w, w_perm)


def _qkv_kernel(q_ref, k_ref, v_ref, qw_ref, kw_ref, g_ref, qo_ref, ko_ref, vo_ref):
    gmat = g_ref[...]

    def headnorm(x, w):
        hi, lo = _split2(x * x)
        ss = jnp.dot(hi, gmat, preferred_element_type=F32) + jnp.dot(lo, gmat, preferred_element_type=F32)
        return x * lax.rsqrt(ss * (1.0 / HEAD_DIM) + EPS) * w

    qw = qw_ref[...]
    for c in range(ATTN_WIDTH // 256):
        sl = slice(c * 256, (c + 1) * 256)
        qo_ref[:, sl] = (headnorm(q_ref[:, sl], qw) * -(HEAD_DIM ** -0.5)).astype(BF16)
    ko_ref[...] = headnorm(k_ref[...], kw_ref[...])
    vo_ref[...] = v_ref[...]


def qkv_post(proj, qw_t, kw_t, gmat, tm=512):
    n = proj.shape[0]
    tm = min(tm, n)
    return pl.pallas_call(
        _qkv_kernel,
        grid=(n // tm,),
        in_specs=[pl.BlockSpec((tm, ATTN_WIDTH), lambda i: (i, COL_Q // ATTN_WIDTH)),
                  pl.BlockSpec((tm, KV_WIDTH), lambda i: (i, COL_K // KV_WIDTH)),
                  pl.BlockSpec((tm, KV_WIDTH), lambda i: (i, COL_V // KV_WIDTH)),
                  pl.BlockSpec((1, 256), lambda i: (0, 0)),
                  pl.BlockSpec((1, 256), lambda i: (0, 0)),
                  pl.BlockSpec((256, 256), lambda i: (0, 0))],
        out_specs=[pl.BlockSpec((tm, ATTN_WIDTH), lambda i: (i, 0)),
                   pl.BlockSpec((tm, KV_WIDTH), lambda i: (i, 0)),
                   pl.BlockSpec((tm, KV_WIDTH), lambda i: (i, 0))],
        out_shape=[jax.ShapeDtypeStruct((n, ATTN_WIDTH), BF16),
                   jax.ShapeDtypeStruct((n, KV_WIDTH), F32),
                   jax.ShapeDtypeStruct((n, KV_WIDTH), F32)],
        compiler_params=_cparams(("parallel",)),
        name="qkv_post",
    )(proj, proj, proj, qw_t, kw_t, gmat)


def _sb_logs(zn, valid):
    l1m = _log1m_beta(zn)
    if valid is not None:
        l1m = jnp.where(valid, l1m, 0.0)
    return l1m.astype(BF16), jnp.sum(l1m, axis=-1, keepdims=True)


def _sb_weights(zn, suffix_incl, valid):
    w = jnp.exp(suffix_incl - zn)
    if valid is not None:
        w = jnp.where(valid, w, 0.0)
    return w.astype(BF16)


def _attn_prompt_kernel(q_ref, b3_ref, kt_ref, v_ref, u_ref, o_ref, zn_a, zn_b, lg_a, lg_b, w_a, w_b, acc_ref):
    i = pl.program_id(2)
    tq = q_ref.shape[1]
    umat = u_ref[...]
    qblk = q_ref[0]
    heads = range(Q_PER_KV)
    qa = [jnp.concatenate([qblk[:, j * HEAD_DIM:(j + 1) * HEAD_DIM],
                           jnp.broadcast_to(b3_ref[0, j:j + 1, :], (tq, HEAD_DIM)).astype(BF16)], axis=1)
          for j in heads]

    zn_ref, lg_ref, w_ref = (zn_a, zn_b), (lg_a, lg_b), (w_a, w_b)

    def stage1(jj, s, cum, valid):
        kt = kt_ref[0, 0, jj]
        new = []
        for j in heads:
            zn = jnp.dot(qa[j], kt, preferred_element_type=F32)
            lg, rs = _sb_logs(zn, valid)
            zn_ref[s][j] = zn
            lg_ref[s][j] = lg
            new.append(cum[j] + rs)
        return cum, new

    def stage2(s_in, s_out, valid):
        for j in heads:
            sfx = jnp.dot(lg_ref[s_in][j], umat, preferred_element_type=F32)
            w_ref[s_out][j] = _sb_weights(zn_ref[s_in][j], sfx, valid)

    def stage3(jj, s_in, c):
        v = v_ref[0, 0, jj]
        for j in heads:
            acc_ref[j] += jnp.exp(c[j]) * jnp.dot(w_ref[s_in][j], v, preferred_element_type=F32)

    row = lax.broadcasted_iota(jnp.int32, (tq, tq), 0)
    col = lax.broadcasted_iota(jnp.int32, (tq, tq), 1)
    diag = col < row
    acc_ref[...] = jnp.zeros(acc_ref.shape, F32)
    cw, cum = stage1(i, 0, [jnp.zeros((tq, 1), F32)] * Q_PER_KV, diag)
    c, cum = stage1(jnp.maximum(i - 1, 0), 1, cum, None)
    stage2(0, 1, diag)

    def trip(t, s, st):
        c, cw, cum = st
        c_n, cum = stage1(jnp.maximum(i - t - 2, 0), s, cum, None)
        stage2(1 - s, s, None)
        stage3(i - t, 1 - s, cw)
        return c_n, c, cum

    c, cw, cum = lax.fori_loop(0, (i + 2) // 2 - 1, lambda u, st: trip(2 * u + 1, 1, trip(2 * u, 0, st)),
                               (c, cw, cum))

    @pl.when(i % 2 == 0)
    def _():
        stage3(0, 1, cw)

    @pl.when(i % 2 == 1)
    def _():
        stage2(1, 0, None)
        stage3(1, 1, cw)
        stage3(0, 0, c)

    o_ref[0] = jnp.concatenate([acc_ref[j] for j in heads], axis=-1).astype(BF16)


def attn_prompt(q_btd, nb3, kt_att, v_att, umat):
    b, t, _ = q_btd.shape
    tq = umat.shape[1]
    nb = t // tq
    gw = Q_PER_KV * HEAD_DIM
    return pl.pallas_call(
        _attn_prompt_kernel,
        grid=(b, N_KV_HEADS, nb),
        in_specs=[pl.BlockSpec((1, tq, gw), lambda bb, g, i: (bb, i, g)),
                  pl.BlockSpec((1, Q_PER_KV, HEAD_DIM), lambda bb, g, i: (g, 0, 0)),
                  pl.BlockSpec((1, 1, nb, 2 * HEAD_DIM, tq), lambda bb, g, i: (bb, g, 0, 0, 0)),
                  pl.BlockSpec((1, 1, nb, tq, HEAD_DIM), lambda bb, g, i: (bb, g, 0, 0, 0)),
                  pl.BlockSpec((tq, tq), lambda bb, g, i: (0, 0))],
        out_specs=pl.BlockSpec((1, tq, gw), lambda bb, g, i: (bb, i, g)),
        out_shape=jax.ShapeDtypeStruct((b, t, ATTN_WIDTH), BF16),
        scratch_shapes=[pltpu.VMEM((Q_PER_KV, tq, tq), F32)] * 2
                       + [pltpu.VMEM((Q_PER_KV, tq, tq), BF16)] * 2
                       + [pltpu.VMEM((Q_PER_KV, tq, tq), BF16)] * 2
                       + [pltpu.VMEM((Q_PER_KV, tq, HEAD_DIM), F32)],
        compiler_params=_cparams(("parallel", "parallel", "parallel")),
        name="attn_prompt",
    )(q_btd, nb3, kt_att, v_att, umat)


def _attn_sample_kernel(pt_ref, q_ref, nbias_ref, knt_ref, vnt_ref, *rest, t_new, npg):
    k_refs = rest[:npg]
    v_refs = rest[npg:2 * npg]
    u_ref, un_ref, o_ref, carry_ref = rest[2 * npg:]
    s = pl.program_id(1)
    q = q_ref[0]
    nbias = nbias_ref[...]
    m = q.shape[0]
    pg = PAGE_SIZE

    nt = (((1,), (1,)), ((), ()))

    @pl.when(s == 0)
    def _():
        zn = jnp.dot(q, knt_ref[0].astype(BF16), preferred_element_type=F32) + nbias[:, 0:pg]
        row = lax.broadcasted_iota(jnp.int32, (m, pg), 0)
        col = lax.broadcasted_iota(jnp.int32, (m, pg), 1)
        valid = col < (row % t_new)
        lg, rs = _sb_logs(zn, valid)
        w = _sb_weights(zn, jnp.dot(lg, un_ref[...], preferred_element_type=F32), valid)
        carry_ref[...] = rs
        o_ref[0] = lax.dot_general(w, vnt_ref[0].astype(BF16), nt, preferred_element_type=F32)

    pph = npg // 2
    nblk = pph // 2
    umat = u_ref[...]

    def logits(h):
        kt = jnp.concatenate([k_refs[p][0].reshape(KV_WIDTH, pg).astype(BF16)
                              for p in range(h * pph, (h + 1) * pph)], axis=1)
        zc = jnp.dot(q, kt, preferred_element_type=F32)
        return jnp.concatenate([zc[:, b * 2 * pg:(b + 1) * 2 * pg] + nbias for b in range(nblk)], axis=0)

    def logs(zn, run):
        lg, rs = _sb_logs(zn, None)
        carries = [None] * nblk
        for b in reversed(range(nblk)):
            carries[b] = run
            run = run + rs[b * m:(b + 1) * m]
        return lg, jnp.concatenate(carries, axis=0), run

    def weighted_values(zn, suffix, h):
        w = _sb_weights(zn, suffix, None)
        w_cat = jnp.concatenate([w[b * m:(b + 1) * m] for b in range(nblk)], axis=1)
        vt = jnp.concatenate([v_refs[p][0].reshape(KV_WIDTH, pg).astype(BF16)
                              for p in range(h * pph, (h + 1) * pph)], axis=1)
        return lax.dot_general(w_cat, vt, nt, preferred_element_type=F32)

    zn_hi = logits(1)
    zn_lo = logits(0)
    lg_hi, c_hi, run = logs(zn_hi, carry_ref[...])
    sfx_hi = jnp.dot(lg_hi, umat, preferred_element_type=F32) + c_hi
    lg_lo, c_lo, run = logs(zn_lo, run)
    sfx_lo = jnp.dot(lg_lo, umat, preferred_element_type=F32) + c_lo
    carry_ref[...] = run
    o_ref[0] += weighted_values(zn_hi, sfx_hi, 1) + weighted_values(zn_lo, sfx_lo, 0)


def attn_sample(page_flat, q_pad, nbias_rep, knt, vnt, cache_kt, cache_vt, umat2, umat1, n_pages, t_new):
    b, m, _ = q_pad.shape
    npg = PAGES_PER_STEP
    nsteps = n_pages // npg

    def page_map(p):
        return lambda bb, s, pt: (pt[bb * n_pages + (nsteps - 1 - s) * npg + p], 0, 0, 0)

    page_specs = [pl.BlockSpec((1, N_KV_HEADS, HEAD_DIM, PAGE_SIZE), page_map(p)) for p in range(npg)]
    grid_spec = pltpu.PrefetchScalarGridSpec(
        num_scalar_prefetch=1,
        grid=(b, nsteps),
        in_specs=[pl.BlockSpec((1, m, KV_WIDTH), lambda bb, s, pt: (bb, 0, 0)),
                  pl.BlockSpec((m, 2 * PAGE_SIZE), lambda bb, s, pt: (0, 0)),
                  pl.BlockSpec((1, KV_WIDTH, PAGE_SIZE), lambda bb, s, pt: (bb, 0, 0)),
                  pl.BlockSpec((1, KV_WIDTH, PAGE_SIZE), lambda bb, s, pt: (bb, 0, 0))]
                 + page_specs + page_specs
                 + [pl.BlockSpec((2 * PAGE_SIZE, 2 * PAGE_SIZE), lambda bb, s, pt: (0, 0)),
                    pl.BlockSpec((PAGE_SIZE, PAGE_SIZE), lambda bb, s, pt: (0, 0))],
        out_specs=pl.BlockSpec((1, m, KV_WIDTH), lambda bb, s, pt: (bb, 0, 0)),
        scratch_shapes=[pltpu.VMEM((m, 1), F32)],
    )
    return pl.pallas_call(
        functools.partial(_attn_sample_kernel, t_new=t_new, npg=npg),
        grid_spec=grid_spec,
        out_shape=jax.ShapeDtypeStruct((b, m, KV_WIDTH), F32),
        compiler_params=_cparams(("parallel", "arbitrary")),
        name="attn_sample",
    )(page_flat, q_pad, nbias_rep, knt, vnt, *([cache_kt] * npg), *([cache_vt] * npg), umat2, umat1)


def _ssd_kernel(*refs, t_valid, has_state):
    it = iter(refs)
    xbc_ref, z_ref, dt_ref, cw_ref, cb_ref, dtb_ref, alog_ref, dsk_ref, nw_ref, tri_ref = (
        [next(it) for _ in range(10)])
    if has_state:
        cs_ref = next(it)
        h0_ref = next(it)
    y_ref, hs_ref, cso_ref, ext_ref, act_ref, ysc_ref = [next(it) for _ in range(6)]
    c = pl.program_id(1)
    ln = SSD_CHUNK
    pad = ln - t_valid

    @pl.when(c == 0)
    def _():
        if has_state:
            ext_ref[0:8, :] = cs_ref[0]
            hs_ref[...] = h0_ref[...]
        else:
            ext_ref[0:8, :] = jnp.zeros((8, CONV_CH), F32)
            hs_ref[...] = jnp.zeros(hs_ref.shape, F32)
        if pad:
            ext_ref[8 + t_valid:, :] = jnp.zeros((pad, CONV_CH), F32)

    ext_ref[8:8 + t_valid, :] = xbc_ref[...]
    cso_ref[0] = ext_ref[t_valid:t_valid + 8, :]

    for cc in range(CONV_CH // 512):
        sl = slice(cc * 512, (cc + 1) * 512)
        acc = cb_ref[:, sl] + cw_ref[3:4, sl] * ext_ref[8:8 + ln, sl]
        for j in range(1, CONV_W):
            acc = acc + cw_ref[3 - j:4 - j, sl] * ext_ref[8 - j:8 - j + ln, sl]
        act_ref[:, sl] = _silu(acc)

    ext_ref[0:8, :] = ext_ref[ln:ln + 8, :]

    row = lax.broadcasted_iota(jnp.int32, (ln, ln), 0)
    col = lax.broadcasted_iota(jnp.int32, (ln, ln), 1)

    dt_raw = dt_ref[...]
    if pad:
        dt_raw = jnp.concatenate([dt_raw, jnp.zeros((pad, DT_PAD), F32)], axis=0)
    xdt = dt_raw + dtb_ref[...]
    dtv = jnp.maximum(xdt, 0.0) + jnp.log1p(jnp.exp(-jnp.abs(xdt)))
    if pad:
        dtv = jnp.where(row < t_valid, dtv, 0.0)
    a_head = jnp.where(col[0:1, :] < SSD_HEADS, -jnp.exp(alog_ref[...]), 0.0)
    a = dtv * a_head

    tri = tri_ref[...]
    a1 = a.astype(BF16)
    r1 = a - a1.astype(F32)
    a2 = r1.astype(BF16)
    a3 = (r1 - a2.astype(F32)).astype(BF16)
    a_cum = (jnp.dot(tri, a1, preferred_element_type=F32) + jnp.dot(tri, a2, preferred_element_type=F32)
             + jnp.dot(tri, a3, preferred_element_type=F32))
    a_cum_t = a_cum.T
    dt_t = dtv.T
    last = jnp.broadcast_to(a_cum_t[:, ln - 1:ln], (ln, ln))
    chunk_decay = jnp.exp(last)
    upd_scale = jnp.exp(last - a_cum_t) * dt_t
    lower = row >= col
    lane_lo = col < SSD_HEAD_DIM
    sub_lo = row < SSD_HEAD_DIM

    for g in range(N_GROUPS):
        b_g = act_ref[:, D_INNER + g * D_STATE:D_INNER + (g + 1) * D_STATE]
        c_g = act_ref[:, D_INNER + N_GROUPS * D_STATE + g * D_STATE:D_INNER + N_GROUPS * D_STATE + (g + 1) * D_STATE]
        b_bf = b_g.astype(BF16)
        cb = lax.dot_general(c_g.astype(BF16), b_bf, (((1,), (1,)), ((), ())), preferred_element_type=F32)
        for pp in range(HEADS_PER_GROUP // 2):
            e0 = g * HEADS_PER_GROUP + 2 * pp
            lanes = slice(e0 * SSD_HEAD_DIM, (e0 + 2) * SSD_HEAD_DIM)
            xs_pair = act_ref[:, lanes]
            xs_bf = xs_pair.astype(BF16)
            h_pair = hs_ref[0, e0:e0 + 2].reshape(2 * SSD_HEAD_DIM, D_STATE)
            h_bf = h_pair.astype(BF16)
            ys = []
            for e in (e0, e0 + 1):
                colb = jnp.broadcast_to(a_cum[:, e:e + 1], (ln, ln))
                seg = colb - a_cum_t[e:e + 1, :]
                lmat = jnp.exp(jnp.where(lower, seg, -jnp.inf))
                m_e = (cb * lmat * dt_t[e:e + 1, :]).astype(BF16)
                c_e = (c_g * jnp.exp(colb)).astype(BF16)
                y_e = jnp.dot(m_e, xs_bf, preferred_element_type=F32)
                y_e = y_e + lax.dot_general(c_e, h_bf, (((1,), (1,)), ((), ())), preferred_element_type=F32)
                ys.append(y_e)
            ysc_ref[:, lanes] = jnp.where(lane_lo, ys[0], ys[1]) + dsk_ref[:, lanes] * xs_pair
            scale = jnp.where(sub_lo, upd_scale[e0:e0 + 1, :], upd_scale[e0 + 1:e0 + 2, :])
            upd = jnp.dot((xs_pair.T * scale).astype(BF16), b_bf, preferred_element_type=F32)
            decay = jnp.where(sub_lo, chunk_decay[e0:e0 + 1, :], chunk_decay[e0 + 1:e0 + 2, :])
            hs_ref[0, e0:e0 + 2] = (h_pair * decay + upd).reshape(2, SSD_HEAD_DIM, D_STATE)

    zz = z_ref[...]
    if pad:
        zz = jnp.concatenate([zz, jnp.zeros((pad, D_INNER), F32)], axis=0)
    gw = D_INNER // N_GROUPS
    for g in range(N_GROUPS):
        sl = slice(g * gw, (g + 1) * gw)
        blk = ysc_ref[:, sl] * _silu(zz[:, sl])
        ms = jnp.mean(blk * blk, axis=-1, keepdims=True)
        out = blk * lax.rsqrt(ms + EPS) * nw_ref[:, sl]
        y_ref[:, sl] = out[0:t_valid].astype(y_ref.dtype)


def ssd_mixer(proj, nb, nchunks, t_valid, conv_w, conv_b, dtb, alog, dsk, nw, tri, conv_state8=None, h0=None):
    has_state = h0 is not None
    n = proj.shape[0]
    row_map = lambda width_idx: (lambda bb, c: (bb * nchunks + c, width_idx))
    const = lambda bb, c: (0, 0)
    in_specs = [pl.BlockSpec((t_valid, CONV_CH), row_map(COL_XBC // CONV_CH)),
                pl.BlockSpec((t_valid, D_INNER), row_map(COL_Z // D_INNER)),
                pl.BlockSpec((t_valid, DT_PAD), row_map(COL_DT // DT_PAD)),
                pl.BlockSpec((CONV_W, CONV_CH), const),
                pl.BlockSpec((1, CONV_CH), const),
                pl.BlockSpec((1, DT_PAD), const),
                pl.BlockSpec((1, DT_PAD), const),
                pl.BlockSpec((1, D_INNER), const),
                pl.BlockSpec((1, D_INNER), const),
                pl.BlockSpec((SSD_CHUNK, SSD_CHUNK), const)]
    args = [proj, proj, proj, conv_w, conv_b, dtb, alog, dsk, nw, tri]
    if has_state:
        in_specs += [pl.BlockSpec((1, 8, CONV_CH), lambda bb, c: (bb, 0, 0)),
                     pl.BlockSpec((1, SSD_HEADS, SSD_HEAD_DIM, D_STATE), lambda bb, c: (bb, 0, 0, 0))]
        args += [conv_state8, h0]
    y_dtype = BF16 if t_valid % 16 == 0 else F32
    return pl.pallas_call(
        functools.partial(_ssd_kernel, t_valid=t_valid, has_state=has_state),
        grid=(nb, nchunks),
        in_specs=in_specs,
        out_specs=[pl.BlockSpec((t_valid, D_INNER), row_map(0)),
                   pl.BlockSpec((1, SSD_HEADS, SSD_HEAD_DIM, D_STATE), lambda bb, c: (bb, 0, 0, 0)),
                   pl.BlockSpec((1, 8, CONV_CH), lambda bb, c: (bb, 0, 0))],
        out_shape=[jax.ShapeDtypeStruct((n, D_INNER), y_dtype),
                   jax.ShapeDtypeStruct((nb, SSD_HEADS, SSD_HEAD_DIM, D_STATE), F32),
                   jax.ShapeDtypeStruct((nb, 8, CONV_CH), F32)],
        scratch_shapes=[pltpu.VMEM((SSD_CHUNK + 8, CONV_CH), F32),
                        pltpu.VMEM((SSD_CHUNK, CONV_CH), F32),
                        pltpu.VMEM((SSD_CHUNK, D_INNER), F32)],
        compiler_params=_cparams(("parallel", "arbitrary")),
        name="ssd_mixer",
    )(*args)


def _merge_kernel(attn_ref, y_ref, ga_ref, gb_ref, x_ref, wpa_ref, wpb_ref, wo_ref, o_ref):
    pa = jnp.dot(attn_ref[...].astype(BF16), wpa_ref[...], preferred_element_type=F32)
    pb = jnp.dot(y_ref[...].astype(BF16), wpb_ref[...], preferred_element_type=F32)
    sa = 1.0 / (1.0 + jnp.exp(-ga_ref[...]))
    sb = 1.0 / (1.0 + jnp.exp(-gb_ref[...]))
    merged = (sa * pa + sb * pb).astype(BF16)
    o_ref[...] = x_ref[...] + jnp.dot(merged, wo_ref[...], preferred_element_type=F32)


def merge_out(attn, y, proj, x2d, w_pa, w_pb, w_out, tm=512):
    n = x2d.shape[0]
    tm = min(tm, n)
    const = lambda i: (0, 0)
    return pl.pallas_call(
        _merge_kernel,
        grid=(n // tm,),
        in_specs=[pl.BlockSpec((tm, ATTN_WIDTH), lambda i: (i, 0)),
                  pl.BlockSpec((tm, D_INNER), lambda i: (i, 0)),
                  pl.BlockSpec((tm, D_MODEL), lambda i: (i, COL_GA // D_MODEL)),
                  pl.BlockSpec((tm, D_MODEL), lambda i: (i, COL_GB // D_MODEL)),
                  pl.BlockSpec((tm, D_MODEL), lambda i: (i, 0)),
                  pl.BlockSpec((ATTN_WIDTH, D_MODEL), const),
                  pl.BlockSpec((D_INNER, D_MODEL), const),
                  pl.BlockSpec((D_MODEL, D_MODEL), const)],
        out_specs=pl.BlockSpec((tm, D_MODEL), lambda i: (i, 0)),
        out_shape=jax.ShapeDtypeStruct((n, D_MODEL), F32),
        compiler_params=_cparams(("parallel",)),
        name="merge_out",
    )(attn, y, proj, proj, x2d, w_pa, w_pb, w_out)


def _ffn_kernel(x_ref, nw_ref, wg_ref, wu_ref, wd_ref, o_ref, h_ref):
    j = pl.program_id(1)

    @pl.when(j == 0)
    def _():
        x = x_ref[...]
        ms = jnp.mean(x * x, axis=-1, keepdims=True)
        h_ref[...] = (x * lax.rsqrt(ms + EPS) * nw_ref[...]).astype(BF16)
        o_ref[...] = x

    h = h_ref[...]
    gate = jnp.dot(h, wg_ref[...], preferred_element_type=F32)
    up = jnp.dot(h, wu_ref[...], preferred_element_type=F32)
    act = (_silu(gate) * up).astype(BF16)
    o_ref[...] += jnp.dot(act, wd_ref[...], preferred_element_type=F32)


def ffn(x2d, norm_w, w_gate, w_up, w_down, tm=512, tf=1408):
    n = x2d.shape[0]
    tm = min(tm, n)
    return pl.pallas_call(
        _ffn_kernel,
        grid=(n // tm, D_FF // tf),
        in_specs=[pl.BlockSpec((tm, D_MODEL), lambda i, j: (i, 0)),
                  pl.BlockSpec((1, D_MODEL), lambda i, j: (0, 0)),
                  pl.BlockSpec((D_MODEL, tf), lambda i, j: (0, j)),
                  pl.BlockSpec((D_MODEL, tf), lambda i, j: (0, j)),
                  pl.BlockSpec((tf, D_MODEL), lambda i, j: (j, 0))],
        out_specs=pl.BlockSpec((tm, D_MODEL), lambda i, j: (i, 0)),
        out_shape=jax.ShapeDtypeStruct((n, D_MODEL), F32),
        scratch_shapes=[pltpu.VMEM((tm, D_MODEL), BF16)],
        compiler_params=_cparams(("parallel", "arbitrary")),
        name="ffn",
    )(x2d, norm_w, w_gate, w_up, w_down)


def _suffix_sum_matrix(n):
    r = np.arange(n)
    return jnp.asarray((r[:, None] >= r[None, :]).astype(np.float32), BF16)


def _lower_incl(n):
    r = np.arange(n)
    return jnp.asarray((r[None, :] <= r[:, None]).astype(np.float32), BF16)


def _head_blockdiag(n, h):
    r = np.arange(n) // h
    return jnp.asarray((r[:, None] == r[None, :]).astype(np.float32), BF16)


def kernel(x_prompt, x_sample, cache_k, cache_v, page_table, state_ssm, state_conv, norm1_w, w_in, q_norm_w,
           k_norm_w, sb_bias, conv_w, conv_b, dt_bias, a_log, d_skip, ssd_norm_w, w_pa, w_pb, w_out, norm2_w,
           w_gate, w_up, w_down):
    bp, tp, _ = x_prompt.shape
    bd, td, _ = x_sample.shape
    n_pages = page_table.shape[1]

    wi = w_in[0]
    o = np.cumsum([0, ATTN_WIDTH, KV_WIDTH, KV_WIDTH, D_INNER, CONV_CH, SSD_HEADS, D_MODEL, D_MODEL])
    sec = [wi[:, o[i]:o[i + 1]] for i in range(8)]
    w_perm = jnp.concatenate([sec[3], sec[0], sec[4], sec[6], sec[7], sec[1], sec[2], sec[5],
                              jnp.zeros((D_MODEL, DT_PAD - SSD_HEADS), F32)], axis=1).astype(BF16)
    n1w = norm1_w[0][None, :]
    n2w = norm2_w[0][None, :]
    qw_t = jnp.tile(q_norm_w[0], 256 // HEAD_DIM)[None, :]
    kw_t = jnp.tile(k_norm_w[0], 256 // HEAD_DIM)[None, :]
    gmat = _head_blockdiag(256, HEAD_DIM)
    nbias = -sb_bias[0]
    cw = conv_w[0]
    cb = conv_b[0][None, :]
    dtb = jnp.pad(dt_bias[0], (0, DT_PAD - SSD_HEADS))[None, :]
    alog = jnp.pad(a_log[0], (0, DT_PAD - SSD_HEADS))[None, :]
    dsk = jnp.repeat(d_skip[0], SSD_HEAD_DIM)[None, :]
    snw = ssd_norm_w[0][None, :]
    tri = _lower_incl(SSD_CHUNK)
    wpa, wpb, wo = w_pa[0].astype(BF16), w_pb[0].astype(BF16), w_out[0].astype(BF16)
    wg, wu, wd = w_gate[0].astype(BF16), w_up[0].astype(BF16), w_down[0].astype(BF16)

    xp2 = x_prompt.reshape(bp * tp, D_MODEL)
    proj_p = in_proj(xp2, n1w, w_perm)
    q_p, k_p, v_p = qkv_post(proj_p, qw_t, kw_t, gmat)
    nbk = tp // ATT_TQ
    k5 = k_p.astype(BF16).reshape(bp, nbk, ATT_TQ, N_KV_HEADS, HEAD_DIM)
    v5 = v_p.astype(BF16).reshape(bp, nbk, ATT_TQ, N_KV_HEADS, HEAD_DIM)
    ones_rows = jnp.zeros((HEAD_DIM, ATT_TQ), BF16).at[0:3].set(1)
    kt_att = jnp.concatenate([k5.transpose(0, 3, 1, 4, 2),
                              jnp.broadcast_to(ones_rows, (bp, N_KV_HEADS, nbk, HEAD_DIM, ATT_TQ))], axis=3)
    v_att = v5.transpose(0, 3, 1, 2, 4)
    b1 = nbias.astype(BF16)
    b2 = (nbias - b1.astype(F32)).astype(BF16)
    b3 = (nbias - b1.astype(F32) - b2.astype(F32)).astype(BF16)
    nb3 = jnp.zeros((N_Q_HEADS, HEAD_DIM), F32).at[:, 0:3].set(jnp.stack([b1, b2, b3], axis=1).astype(F32))
    attn_p = attn_prompt(q_p.reshape(bp, tp, ATTN_WIDTH), nb3.reshape(N_KV_HEADS, Q_PER_KV, HEAD_DIM), kt_att, v_att,
                         _suffix_sum_matrix(ATT_TQ)).reshape(bp * tp, ATTN_WIDTH)
    y_p, ssm_p, conv_p8 = ssd_mixer(proj_p, bp, tp // SSD_CHUNK, SSD_CHUNK, cw, cb, dtb, alog, dsk, snw, tri)
    x1_p = merge_out(attn_p, y_p, proj_p, xp2, wpa, wpb, wo)
    out_p = ffn(x1_p, n2w, wg, wu, wd).reshape(bp, tp, D_MODEL)

    xs2 = x_sample.reshape(bd * td, D_MODEL)
    proj_s = in_proj(xs2, n1w, w_perm)
    q_s, k_s, v_s = qkv_post(proj_s, qw_t, kw_t, gmat)
    m = N_Q_HEADS * td
    q_t = q_s.reshape(bd, td, N_Q_HEADS, HEAD_DIM).transpose(0, 2, 1, 3)
    onehot = jnp.asarray(np.arange(N_Q_HEADS)[:, None] // Q_PER_KV == np.arange(N_KV_HEADS)[None, :], BF16)
    q_pad = (q_t[:, :, :, None, :] * onehot[None, :, None, :, None]).reshape(bd, m, KV_WIDTH)
    nbias_rep = jnp.broadcast_to(jnp.repeat(nbias, td)[:, None], (m, 2 * PAGE_SIZE))
    lane_pad = ((0, 0), (0, 0), (0, PAGE_SIZE - td))
    knt = jnp.pad(k_s.reshape(bd, td, KV_WIDTH).transpose(0, 2, 1), lane_pad)
    vnt = jnp.pad(v_s.reshape(bd, td, KV_WIDTH).transpose(0, 2, 1), lane_pad)
    ckt = cache_k[0].transpose(0, 2, 3, 1)
    cvt = cache_v[0].transpose(0, 2, 3, 1)
    o_s = attn_sample(page_table.reshape(-1), q_pad, nbias_rep, knt, vnt, ckt, cvt,
                      _suffix_sum_matrix(2 * PAGE_SIZE), _suffix_sum_matrix(PAGE_SIZE), n_pages, td)
    o_s = o_s.reshape(bd, N_KV_HEADS, Q_PER_KV, td, N_KV_HEADS, HEAD_DIM)
    attn_s = jnp.einsum('bgjtgd->btgjd', o_s).reshape(bd * td, ATTN_WIDTH).astype(BF16)
    cs8 = jnp.pad(state_conv[0], ((0, 0), (8 - (CONV_W - 1), 0), (0, 0)))
    y_s, ssm_s, conv_s8 = ssd_mixer(proj_s, bd, 1, td, cw, cb, dtb, alog, dsk, snw, tri, cs8, state_ssm[0])
    x1_s = merge_out(attn_s, y_s, proj_s, xs2, wpa, wpb, wo)
    out_s = ffn(x1_s, n2w, wg, wu, wd).reshape(bd, td, D_MODEL)

    kvp = (1, bp, tp, N_KV_HEADS, HEAD_DIM)
    kvs = (1, bd, td, N_KV_HEADS, HEAD_DIM)
    return (out_p, out_s, k_p.reshape(kvp), v_p.reshape(kvp), ssm_p[None], conv_p8[None, :, 8 - (CONV_W - 1):, :],
            k_s.reshape(kvs), v_s.reshape(kvs), ssm_s[None], conv_s8[None, :, 8 - (CONV_W - 1):, :])
```

```python
import functools

import jax
import jax.numpy as jnp
import numpy as np
from jax import lax
from jax.experimental import pallas as pl
from jax.experimental.pallas import tpu as pltpu

F32 = jnp.float32
BF16 = jnp.bfloat16

D_MODEL = 1024
HEAD_DIM = 64
N_Q_HEADS = 16
N_KV_HEADS = 4
Q_PER_KV = 4
ATTN_WIDTH = 1024
KV_WIDTH = 256
D_INNER = 2048
SSD_HEAD_DIM = 64
SSD_HEADS = 32
N_GROUPS = 4
HEADS_PER_GROUP = 8
D_STATE = 128
CONV_W = 4
CONV_CH = 3072
SSD_CHUNK = 128
D_FF = 2816
EPS = 1e-6
PAGE_SIZE = 128

COL_Z, COL_Q, COL_XBC, COL_GA, COL_GB, COL_K, COL_V, COL_DT = 0, 2048, 3072, 6144, 7168, 8192, 8448, 8704
DT_PAD = 128
PROJ_W = COL_DT + DT_PAD

VMEM_LIMIT = 56 * 1024 * 1024
ATT_TQ = 256
PAGES_PER_STEP = 16


def _cparams(sem):
    return pltpu.CompilerParams(dimension_semantics=sem, vmem_limit_bytes=VMEM_LIMIT)


def _split2(x):
    hi = x.astype(BF16)
    lo = (x - hi.astype(F32)).astype(BF16)
    return hi, lo


def _log1m_beta(zn):
    neg_abs = lax.bitcast_convert_type(lax.bitcast_convert_type(zn, jnp.uint32) | jnp.uint32(0x80000000), F32)
    return jnp.minimum(zn, 0.0) - jnp.log(1.0 + jnp.exp(neg_abs))


def _silu(x):
    return x * (1.0 / (1.0 + jnp.exp(-x)))


def _inproj_kernel(x_ref, nw_ref, w_ref, o_ref):
    x = x_ref[...]
    ms = jnp.mean(x * x, axis=-1, keepdims=True)
    xn = (x * lax.rsqrt(ms + EPS) * nw_ref[...]).astype(BF16)
    o_ref[...] = jnp.dot(xn, w_ref[...], preferred_element_type=F32)


def in_proj(x2d, norm_w, w_perm, tm=512, tn=2944):
    n, d = x2d.shape
    tm = min(tm, n)
    w = w_perm.shape[1]
    return pl.pallas_call(
        _inproj_kernel,
        grid=(w // tn, n // tm),
        in_specs=[pl.BlockSpec((tm, d), lambda j, i: (i, 0)),
                  pl.BlockSpec((1, d), lambda j, i: (0, 0)),
                  pl.BlockSpec((d, tn), lambda j, i: (0, j))],
        out_specs=pl.BlockSpec((tm, tn), lambda j, i: (i, j)),
        out_shape=jax.ShapeDtypeStruct((n, w), F32),
        compiler_params=_cparams(("parallel", "parallel")),
        name="in_proj",
    )(x2d, norm_w, w_perm)


def _qkv_kernel(q_ref, k_ref, v_ref, qw_ref, kw_ref, g_ref, qo_ref, ko_ref, vo_ref):
    gmat = g_ref[...]

    def headnorm(x, w):
        hi, lo = _split2(x * x)
        ss = jnp.dot(hi, gmat, preferred_element_type=F32) + jnp.dot(lo, gmat, preferred_element_type=F32)
        return x * lax.rsqrt(ss * (1.0 / HEAD_DIM) + EPS) * w

    qw = qw_ref[...]
    for c in range(ATTN_WIDTH // 256):
        sl = slice(c * 256, (c + 1) * 256)
        qo_ref[:, sl] = (headnorm(q_ref[:, sl], qw) * -(HEAD_DIM ** -0.5)).astype(BF16)
    ko_ref[...] = headnorm(k_ref[...], kw_ref[...])
    vo_ref[...] = v_ref[...]


def qkv_post(proj, qw_t, kw_t, gmat, tm=512):
    n = proj.shape[0]
    tm = min(tm, n)
    return pl.pallas_call(
        _qkv_kernel,
        grid=(n // tm,),
        in_specs=[pl.BlockSpec((tm, ATTN_WIDTH), lambda i: (i, COL_Q // ATTN_WIDTH)),
                  pl.BlockSpec((tm, KV_WIDTH), lambda i: (i, COL_K // KV_WIDTH)),
                  pl.BlockSpec((tm, KV_WIDTH), lambda i: (i, COL_V // KV_WIDTH)),
                  pl.BlockSpec((1, 256), lambda i: (0, 0)),
                  pl.BlockSpec((1, 256), lambda i: (0, 0)),
                  pl.BlockSpec((256, 256), lambda i: (0, 0))],
        out_specs=[pl.BlockSpec((tm, ATTN_WIDTH), lambda i: (i, 0)),
                   pl.BlockSpec((tm, KV_WIDTH), lambda i: (i, 0)),
                   pl.BlockSpec((tm, KV_WIDTH), lambda i: (i, 0))],
        out_shape=[jax.ShapeDtypeStruct((n, ATTN_WIDTH), BF16),
                   jax.ShapeDtypeStruct((n, KV_WIDTH), F32),
                   jax.ShapeDtypeStruct((n, KV_WIDTH), F32)],
        compiler_params=_cparams(("parallel",)),
        name="qkv_post",
    )(proj, proj, proj, qw_t, kw_t, gmat)


def _sb_logs(zn, valid):
    l1m = _log1m_beta(zn)
    if valid is not None:
        l1m = jnp.where(valid, l1m, 0.0)
    return l1m.astype(BF16), jnp.sum(l1m, axis=-1, keepdims=True)


def _sb_weights(zn, suffix_incl, valid):
    w = jnp.exp(suffix_incl - zn)
    if valid is not None:
        w = jnp.where(valid, w, 0.0)
    return w.astype(BF16)


def _attn_prompt_kernel(q_ref, b3_ref, kt_ref, v_ref, u_ref, o_ref, zn_a, zn_b, lg_a, lg_b, w_a, w_b, acc_ref):
    i = pl.program_id(2)
    tq = q_ref.shape[1]
    umat = u_ref[...]
    qblk = q_ref[0]
    heads = range(Q_PER_KV)
    qa = [jnp.concatenate([qblk[:, j * HEAD_DIM:(j + 1) * HEAD_DIM],
                           jnp.broadcast_to(b3_ref[0, j:j + 1, :], (tq, HEAD_DIM)).astype(BF16)], axis=1)
          for j in heads]

    zn_ref, lg_ref, w_ref = (zn_a, zn_b), (lg_a, lg_b), (w_a, w_b)

    def stage1(jj, s, cum, valid):
        kt = kt_ref[0, 0, jj]
        new = []
        for j in heads:
            zn = jnp.dot(qa[j], kt, preferred_element_type=F32)
            lg, rs = _sb_logs(zn, valid)
            zn_ref[s][j] = zn
            lg_ref[s][j] = lg
            new.append(cum[j] + rs)
        return cum, new

    def stage2(s_in, s_out, valid, c):
        for j in heads:
            sfx = jnp.dot(lg_ref[s_in][j], umat, preferred_element_type=F32) + c[j]
            w_ref[s_out][j] = _sb_weights(zn_ref[s_in][j], sfx, valid)

    def stage3(jj, s_in):
        v = v_ref[0, 0, jj]
        for j in heads:
            acc_ref[j] += jnp.dot(w_ref[s_in][j], v, preferred_element_type=F32)

    row = lax.broadcasted_iota(jnp.int32, (tq, tq), 0)
    col = lax.broadcasted_iota(jnp.int32, (tq, tq), 1)
    diag = col < row
    acc_ref[...] = jnp.zeros(acc_ref.shape, F32)
    cw, cum = stage1(i, 0, [jnp.zeros((tq, 1), F32)] * Q_PER_KV, diag)
    c, cum = stage1(jnp.maximum(i - 1, 0), 1, cum, None)
    stage2(0, 1, diag, cw)

    def trip(t, s, st):
        c, cum = st
        c_n, cum = stage1(jnp.maximum(i - t - 2, 0), s, cum, None)
        stage2(1 - s, s, None, c)
        stage3(i - t, 1 - s)
        return c_n, cum

    c, cum = lax.fori_loop(0, (i + 2) // 2 - 1, lambda u, st: trip(2 * u + 1, 1, trip(2 * u, 0, st)), (c, cum))

    @pl.when(i % 2 == 0)
    def _():
        stage3(0, 1)

    @pl.when(i % 2 == 1)
    def _():
        stage2(1, 0, None, c)
        stage3(1, 1)
        stage3(0, 0)

    o_ref[0] = jnp.concatenate([acc_ref[j] for j in heads], axis=-1).astype(BF16)


def attn_prompt(q_btd, nb3, kt_att, v_att, umat):
    b, t, _ = q_btd.shape
    tq = umat.shape[1]
    nb = t // tq
    gw = Q_PER_KV * HEAD_DIM
    return pl.pallas_call(
        _attn_prompt_kernel,
        grid=(b, N_KV_HEADS, nb),
        in_specs=[pl.BlockSpec((1, tq, gw), lambda bb, g, i: (bb, i, g)),
                  pl.BlockSpec((1, Q_PER_KV, HEAD_DIM), lambda bb, g, i: (g, 0, 0)),
                  pl.BlockSpec((1, 1, nb, 2 * HEAD_DIM, tq), lambda bb, g, i: (bb, g, 0, 0, 0)),
                  pl.BlockSpec((1, 1, nb, tq, HEAD_DIM), lambda bb, g, i: (bb, g, 0, 0, 0)),
                  pl.BlockSpec((tq, tq), lambda bb, g, i: (0, 0))],
        out_specs=pl.BlockSpec((1, tq, gw), lambda bb, g, i: (bb, i, g)),
        out_shape=jax.ShapeDtypeStruct((b, t, ATTN_WIDTH), BF16),
        scratch_shapes=[pltpu.VMEM((Q_PER_KV, tq, tq), F32)] * 2
                       + [pltpu.VMEM((Q_PER_KV, tq, tq), BF16)] * 2
                       + [pltpu.VMEM((Q_PER_KV, tq, tq), BF16)] * 2
                       + [pltpu.VMEM((Q_PER_KV, tq, HEAD_DIM), F32)],
        compiler_params=_cparams(("parallel", "parallel", "parallel")),
        name="attn_prompt",
    )(q_btd, nb3, kt_att, v_att, umat)


def _attn_sample_kernel(pt_ref, q_ref, nbias_ref, knt_ref, vnt_ref, *rest, t_new, npg):
    k_refs = rest[:npg]
    v_refs = rest[npg:2 * npg]
    u_ref, un_ref, o_ref, carry_ref = rest[2 * npg:]
    s = pl.program_id(1)
    q = q_ref[0]
    nbias = nbias_ref[...]
    m = q.shape[0]
    pg = PAGE_SIZE

    nt = (((1,), (1,)), ((), ()))

    @pl.when(s == 0)
    def _():
        zn = jnp.dot(q, knt_ref[0].astype(BF16), preferred_element_type=F32) + nbias[:, 0:pg]
        row = lax.broadcasted_iota(jnp.int32, (m, pg), 0)
        col = lax.broadcasted_iota(jnp.int32, (m, pg), 1)
        valid = col < (row % t_new)
        lg, rs = _sb_logs(zn, valid)
        w = _sb_weights(zn, jnp.dot(lg, un_ref[...], preferred_element_type=F32), valid)
        carry_ref[...] = rs
        o_ref[0] = lax.dot_general(w, vnt_ref[0].astype(BF16), nt, preferred_element_type=F32)

    pph = npg // 2
    nblk = pph // 2
    umat = u_ref[...]

    def logits(h):
        kt = jnp.concatenate([k_refs[p][0].reshape(KV_WIDTH, pg).astype(BF16)
                              for p in range(h * pph, (h + 1) * pph)], axis=1)
        zc = jnp.dot(q, kt, preferred_element_type=F32)
        return jnp.concatenate([zc[:, b * 2 * pg:(b + 1) * 2 * pg] + nbias for b in range(nblk)], axis=0)

    def logs(zn, run):
        lg, rs = _sb_logs(zn, None)
        carries = [None] * nblk
        for b in reversed(range(nblk)):
            carries[b] = run
            run = run + rs[b * m:(b + 1) * m]
        return lg, jnp.concatenate(carries, axis=0), run

    def weighted_values(zn, suffix, h):
        w = _sb_weights(zn, suffix, None)
        w_cat = jnp.concatenate([w[b * m:(b + 1) * m] for b in range(nblk)], axis=1)
        vt = jnp.concatenate([v_refs[p][0].reshape(KV_WIDTH, pg).astype(BF16)
                              for p in range(h * pph, (h + 1) * pph)], axis=1)
        return lax.dot_general(w_cat, vt, nt, preferred_element_type=F32)

    zn_hi = logits(1)
    zn_lo = logits(0)
    lg_hi, c_hi, run = logs(zn_hi, carry_ref[...])
    sfx_hi = jnp.dot(lg_hi, umat, preferred_element_type=F32) + c_hi
    lg_lo, c_lo, run = logs(zn_lo, run)
    sfx_lo = jnp.dot(lg_lo, umat, preferred_element_type=F32) + c_lo
    carry_ref[...] = run
    o_ref[0] += weighted_values(zn_hi, sfx_hi, 1) + weighted_values(zn_lo, sfx_lo, 0)


def attn_sample(page_flat, q_pad, nbias_rep, knt, vnt, cache_kt, cache_vt, umat2, umat1, n_pages, t_new):
    b, m, _ = q_pad.shape
    npg = PAGES_PER_STEP
    nsteps = n_pages // npg

    def page_map(p):
        return lambda bb, s, pt: (pt[bb * n_pages + (nsteps - 1 - s) * npg + p], 0, 0, 0)

    page_specs = [pl.BlockSpec((1, N_KV_HEADS, HEAD_DIM, PAGE_SIZE), page_map(p)) for p in range(npg)]
    grid_spec = pltpu.PrefetchScalarGridSpec(
        num_scalar_prefetch=1,
        grid=(b, nsteps),
        in_specs=[pl.BlockSpec((1, m, KV_WIDTH), lambda bb, s, pt: (bb, 0, 0)),
                  pl.BlockSpec((m, 2 * PAGE_SIZE), lambda bb, s, pt: (0, 0)),
                  pl.BlockSpec((1, KV_WIDTH, PAGE_SIZE), lambda bb, s, pt: (bb, 0, 0)),
                  pl.BlockSpec((1, KV_WIDTH, PAGE_SIZE), lambda bb, s, pt: (bb, 0, 0))]
                 + page_specs + page_specs
                 + [pl.BlockSpec((2 * PAGE_SIZE, 2 * PAGE_SIZE), lambda bb, s, pt: (0, 0)),
                    pl.BlockSpec((PAGE_SIZE, PAGE_SIZE), lambda bb, s, pt: (0, 0))],
        out_specs=pl.BlockSpec((1, m, KV_WIDTH), lambda bb, s, pt: (bb, 0, 0)),
        scratch_shapes=[pltpu.VMEM((m, 1), F32)],
    )
    return pl.pallas_call(
        functools.partial(_attn_sample_kernel, t_new=t_new, npg=npg),
        grid_spec=grid_spec,
        out_shape=jax.ShapeDtypeStruct((b, m, KV_WIDTH), F32),
        compiler_params=_cparams(("parallel", "arbitrary")),
        name="attn_sample",
    )(page_flat, q_pad, nbias_rep, knt, vnt, *([cache_kt] * npg), *([cache_vt] * npg), umat2, umat1)


def _ssd_kernel(*refs, t_valid, has_state):
    it = iter(refs)
    xbc_ref, z_ref, dt_ref, cw_ref, cb_ref, dtb_ref, alog_ref, dsk_ref, nw_ref, tri_ref = (
        [next(it) for _ in range(10)])
    if has_state:
        cs_ref = next(it)
        h0_ref = next(it)
    y_ref, hs_ref, cso_ref, ext_ref, act_ref, ysc_ref = [next(it) for _ in range(6)]
    c = pl.program_id(1)
    ln = SSD_CHUNK
    pad = ln - t_valid

    @pl.when(c == 0)
    def _():
        if has_state:
            ext_ref[0:8, :] = cs_ref[0]
            hs_ref[...] = h0_ref[...]
        else:
            ext_ref[0:8, :] = jnp.zeros((8, CONV_CH), F32)
            hs_ref[...] = jnp.zeros(hs_ref.shape, F32)
        if pad:
            ext_ref[8 + t_valid:, :] = jnp.zeros((pad, CONV_CH), F32)

    ext_ref[8:8 + t_valid, :] = xbc_ref[...]
    cso_ref[0] = ext_ref[t_valid:t_valid + 8, :]

    for cc in range(CONV_CH // 512):
        sl = slice(cc * 512, (cc + 1) * 512)
        acc = cb_ref[:, sl] + cw_ref[3:4, sl] * ext_ref[8:8 + ln, sl]
        for j in range(1, CONV_W):
            acc = acc + cw_ref[3 - j:4 - j, sl] * ext_ref[8 - j:8 - j + ln, sl]
        act_ref[:, sl] = _silu(acc)

    ext_ref[0:8, :] = ext_ref[ln:ln + 8, :]

    row = lax.broadcasted_iota(jnp.int32, (ln, ln), 0)
    col = lax.broadcasted_iota(jnp.int32, (ln, ln), 1)

    dt_raw = dt_ref[...]
    if pad:
        dt_raw = jnp.concatenate([dt_raw, jnp.zeros((pad, DT_PAD), F32)], axis=0)
    xdt = dt_raw + dtb_ref[...]
    dtv = jnp.maximum(xdt, 0.0) + jnp.log1p(jnp.exp(-jnp.abs(xdt)))
    if pad:
        dtv = jnp.where(row < t_valid, dtv, 0.0)
    a_head = jnp.where(col[0:1, :] < SSD_HEADS, -jnp.exp(alog_ref[...]), 0.0)
    a = dtv * a_head

    tri = tri_ref[...]
    a1 = a.astype(BF16)
    r1 = a - a1.astype(F32)
    a2 = r1.astype(BF16)
    a3 = (r1 - a2.astype(F32)).astype(BF16)
    a_cum = (jnp.dot(tri, a1, preferred_element_type=F32) + jnp.dot(tri, a2, preferred_element_type=F32)
             + jnp.dot(tri, a3, preferred_element_type=F32))
    a_cum_t = a_cum.T
    dt_t = dtv.T
    last = jnp.broadcast_to(a_cum_t[:, ln - 1:ln], (ln, ln))
    chunk_decay = jnp.exp(last)
    upd_scale = jnp.exp(last - a_cum_t) * dt_t
    lower = row >= col
    lane_lo = col < SSD_HEAD_DIM
    sub_lo = row < SSD_HEAD_DIM

    for g in range(N_GROUPS):
        b_g = act_ref[:, D_INNER + g * D_STATE:D_INNER + (g + 1) * D_STATE]
        c_g = act_ref[:, D_INNER + N_GROUPS * D_STATE + g * D_STATE:D_INNER + N_GROUPS * D_STATE + (g + 1) * D_STATE]
        b_bf = b_g.astype(BF16)
        cb = lax.dot_general(c_g.astype(BF16), b_bf, (((1,), (1,)), ((), ())), preferred_element_type=F32)
        for pp in range(HEADS_PER_GROUP // 2):
            e0 = g * HEADS_PER_GROUP + 2 * pp
            lanes = slice(e0 * SSD_HEAD_DIM, (e0 + 2) * SSD_HEAD_DIM)
            xs_pair = act_ref[:, lanes]
            xs_bf = xs_pair.astype(BF16)
            h_pair = hs_ref[0, e0:e0 + 2].reshape(2 * SSD_HEAD_DIM, D_STATE)
            h_bf = h_pair.astype(BF16)
            ys = []
            for e in (e0, e0 + 1):
                colb = jnp.broadcast_to(a_cum[:, e:e + 1], (ln, ln))
                seg = colb - a_cum_t[e:e + 1, :]
                lmat = jnp.exp(jnp.where(lower, seg, -jnp.inf))
                m_e = (cb * lmat * dt_t[e:e + 1, :]).astype(BF16)
                c_e = (c_g * jnp.exp(colb)).astype(BF16)
                y_e = jnp.dot(m_e, xs_bf, preferred_element_type=F32)
                y_e = y_e + lax.dot_general(c_e, h_bf, (((1,), (1,)), ((), ())), preferred_element_type=F32)
                ys.append(y_e)
            ysc_ref[:, lanes] = jnp.where(lane_lo, ys[0], ys[1]) + dsk_ref[:, lanes] * xs_pair
            scale = jnp.where(sub_lo, upd_scale[e0:e0 + 1, :], upd_scale[e0 + 1:e0 + 2, :])
            upd = jnp.dot((xs_pair.T * scale).astype(BF16), b_bf, preferred_element_type=F32)
            decay = jnp.where(sub_lo, chunk_decay[e0:e0 + 1, :], chunk_decay[e0 + 1:e0 + 2, :])
            hs_ref[0, e0:e0 + 2] = (h_pair * decay + upd).reshape(2, SSD_HEAD_DIM, D_STATE)

    zz = z_ref[...]
    if pad:
        zz = jnp.concatenate([zz, jnp.zeros((pad, D_INNER), F32)], axis=0)
    gw = D_INNER // N_GROUPS
    for g in range(N_GROUPS):
        sl = slice(g * gw, (g + 1) * gw)
        blk = ysc_ref[:, sl] * _silu(zz[:, sl])
        ms = jnp.mean(blk * blk, axis=-1, keepdims=True)
        out = blk * lax.rsqrt(ms + EPS) * nw_ref[:, sl]
        y_ref[:, sl] = out[0:t_valid].astype(y_ref.dtype)


def ssd_mixer(proj, nb, nchunks, t_valid, conv_w, conv_b, dtb, alog, dsk, nw, tri, conv_state8=None, h0=None):
    has_state = h0 is not None
    n = proj.shape[0]
    row_map = lambda width_idx: (lambda bb, c: (bb * nchunks + c, width_idx))
    const = lambda bb, c: (0, 0)
    in_specs = [pl.BlockSpec((t_valid, CONV_CH), row_map(COL_XBC // CONV_CH)),
                pl.BlockSpec((t_valid, D_INNER), row_map(COL_Z // D_INNER)),
                pl.BlockSpec((t_valid, DT_PAD), row_map(COL_DT // DT_PAD)),
                pl.BlockSpec((CONV_W, CONV_CH), const),
                pl.BlockSpec((1, CONV_CH), const),
                pl.BlockSpec((1, DT_PAD), const),
                pl.BlockSpec((1, DT_PAD), const),
                pl.BlockSpec((1, D_INNER), const),
                pl.BlockSpec((1, D_INNER), const),
                pl.BlockSpec((SSD_CHUNK, SSD_CHUNK), const)]
    args = [proj, proj, proj, conv_w, conv_b, dtb, alog, dsk, nw, tri]
    if has_state:
        in_specs += [pl.BlockSpec((1, 8, CONV_CH), lambda bb, c: (bb, 0, 0)),
                     pl.BlockSpec((1, SSD_HEADS, SSD_HEAD_DIM, D_STATE), lambda bb, c: (bb, 0, 0, 0))]
        args += [conv_state8, h0]
    y_dtype = BF16 if t_valid % 16 == 0 else F32
    return pl.pallas_call(
        functools.partial(_ssd_kernel, t_valid=t_valid, has_state=has_state),
        grid=(nb, nchunks),
        in_specs=in_specs,
        out_specs=[pl.BlockSpec((t_valid, D_INNER), row_map(0)),
                   pl.BlockSpec((1, SSD_HEADS, SSD_HEAD_DIM, D_STATE), lambda bb, c: (bb, 0, 0, 0)),
                   pl.BlockSpec((1, 8, CONV_CH), lambda bb, c: (bb, 0, 0))],
        out_shape=[jax.ShapeDtypeStruct((n, D_INNER), y_dtype),
                   jax.ShapeDtypeStruct((nb, SSD_HEADS, SSD_HEAD_DIM, D_STATE), F32),
                   jax.ShapeDtypeStruct((nb, 8, CONV_CH), F32)],
        scratch_shapes=[pltpu.VMEM((SSD_CHUNK + 8, CONV_CH), F32),
                        pltpu.VMEM((SSD_CHUNK, CONV_CH), F32),
                        pltpu.VMEM((SSD_CHUNK, D_INNER), F32)],
        compiler_params=_cparams(("parallel", "arbitrary")),
        name="ssd_mixer",
    )(*args)


def _merge_kernel(attn_ref, y_ref, ga_ref, gb_ref, x_ref, wpa_ref, wpb_ref, wo_ref, o_ref):
    pa = jnp.dot(attn_ref[...].astype(BF16), wpa_ref[...], preferred_element_type=F32)
    pb = jnp.dot(y_ref[...].astype(BF16), wpb_ref[...], preferred_element_type=F32)
    sa = 1.0 / (1.0 + jnp.exp(-ga_ref[...]))
    sb = 1.0 / (1.0 + jnp.exp(-gb_ref[...]))
    merged = (sa * pa + sb * pb).astype(BF16)
    o_ref[...] = x_ref[...] + jnp.dot(merged, wo_ref[...], preferred_element_type=F32)


def merge_out(attn, y, proj, x2d, w_pa, w_pb, w_out, tm=512):
    n = x2d.shape[0]
    tm = min(tm, n)
    const = lambda i: (0, 0)
    return pl.pallas_call(
        _merge_kernel,
        grid=(n // tm,),
        in_specs=[pl.BlockSpec((tm, ATTN_WIDTH), lambda i: (i, 0)),
                  pl.BlockSpec((tm, D_INNER), lambda i: (i, 0)),
                  pl.BlockSpec((tm, D_MODEL), lambda i: (i, COL_GA // D_MODEL)),
                  pl.BlockSpec((tm, D_MODEL), lambda i: (i, COL_GB // D_MODEL)),
                  pl.BlockSpec((tm, D_MODEL), lambda i: (i, 0)),
                  pl.BlockSpec((ATTN_WIDTH, D_MODEL), const),
                  pl.BlockSpec((D_INNER, D_MODEL), const),
                  pl.BlockSpec((D_MODEL, D_MODEL), const)],
        out_specs=pl.BlockSpec((tm, D_MODEL), lambda i: (i, 0)),
        out_shape=jax.ShapeDtypeStruct((n, D_MODEL), F32),
        compiler_params=_cparams(("parallel",)),
        name="merge_out",
    )(attn, y, proj, proj, x2d, w_pa, w_pb, w_out)


def _ffn_kernel(x_ref, nw_ref, wg_ref, wu_ref, wd_ref, o_ref, h_ref):
    j = pl.program_id(1)

    @pl.when(j == 0)
    def _():
        x = x_ref[...]
        ms = jnp.mean(x * x, axis=-1, keepdims=True)
        h_ref[...] = (x * lax.rsqrt(ms + EPS) * nw_ref[...]).astype(BF16)
        o_ref[...] = x

    h = h_ref[...]
    gate = jnp.dot(h, wg_ref[...], preferred_element_type=F32)
    up = jnp.dot(h, wu_ref[...], preferred_element_type=F32)
    act = (_silu(gate) * up).astype(BF16)
    o_ref[...] += jnp.dot(act, wd_ref[...], preferred_element_type=F32)


def ffn(x2d, norm_w, w_gate, w_up, w_down, tm=512, tf=1408):
    n = x2d.shape[0]
    tm = min(tm, n)
    return pl.pallas_call(
        _ffn_kernel,
        grid=(n // tm, D_FF // tf),
        in_specs=[pl.BlockSpec((tm, D_MODEL), lambda i, j: (i, 0)),
                  pl.BlockSpec((1, D_MODEL), lambda i, j: (0, 0)),
                  pl.BlockSpec((D_MODEL, tf), lambda i, j: (0, j)),
                  pl.BlockSpec((D_MODEL, tf), lambda i, j: (0, j)),
                  pl.BlockSpec((tf, D_MODEL), lambda i, j: (j, 0))],
        out_specs=pl.BlockSpec((tm, D_MODEL), lambda i, j: (i, 0)),
        out_shape=jax.ShapeDtypeStruct((n, D_MODEL), F32),
        scratch_shapes=[pltpu.VMEM((tm, D_MODEL), BF16)],
        compiler_params=_cparams(("parallel", "arbitrary")),
        name="ffn",
    )(x2d, norm_w, w_gate, w_up, w_down)


def _suffix_sum_matrix(n):
    r = np.arange(n)
    return jnp.asarray((r[:, None] >= r[None, :]).astype(np.float32), BF16)


def _lower_incl(n):
    r = np.arange(n)
    return jnp.asarray((r[None, :] <= r[:, None]).astype(np.float32), BF16)


def _head_blockdiag(n, h):
    r = np.arange(n) // h
    return jnp.asarray((r[:, None] == r[None, :]).astype(np.float32), BF16)


def kernel(x_prompt, x_sample, cache_k, cache_v, page_table, state_ssm, state_conv, norm1_w, w_in, q_norm_w,
           k_norm_w, sb_bias, conv_w, conv_b, dt_bias, a_log, d_skip, ssd_norm_w, w_pa, w_pb, w_out, norm2_w,
           w_gate, w_up, w_down):
    bp, tp, _ = x_prompt.shape
    bd, td, _ = x_sample.shape
    n_pages = page_table.shape[1]

    wi = w_in[0]
    o = np.cumsum([0, ATTN_WIDTH, KV_WIDTH, KV_WIDTH, D_INNER, CONV_CH, SSD_HEADS, D_MODEL, D_MODEL])
    sec = [wi[:, o[i]:o[i + 1]] for i in range(8)]
    w_perm = jnp.concatenate([sec[3], sec[0], sec[4], sec[6], sec[7], sec[1], sec[2], sec[5],
                              jnp.zeros((D_MODEL, DT_PAD - SSD_HEADS), F32)], axis=1).astype(BF16)
    n1w = norm1_w[0][None, :]
    n2w = norm2_w[0][None, :]
    qw_t = jnp.tile(q_norm_w[0], 256 // HEAD_DIM)[None, :]
    kw_t = jnp.tile(k_norm_w[0], 256 // HEAD_DIM)[None, :]
    gmat = _head_blockdiag(256, HEAD_DIM)
    nbias = -sb_bias[0]
    cw = conv_w[0]
    cb = conv_b[0][None, :]
    dtb = jnp.pad(dt_bias[0], (0, DT_PAD - SSD_HEADS))[None, :]
    alog = jnp.pad(a_log[0], (0, DT_PAD - SSD_HEADS))[None, :]
    dsk = jnp.repeat(d_skip[0], SSD_HEAD_DIM)[None, :]
    snw = ssd_norm_w[0][None, :]
    tri = _lower_incl(SSD_CHUNK)
    wpa, wpb, wo = w_pa[0].astype(BF16), w_pb[0].astype(BF16), w_out[0].astype(BF16)
    wg, wu, wd = w_gate[0].astype(BF16), w_up[0].astype(BF16), w_down[0].astype(BF16)

    xp2 = x_prompt.reshape(bp * tp, D_MODEL)
    proj_p = in_proj(xp2, n1w, w_perm)
    q_p, k_p, v_p = qkv_post(proj_p, qw_t, kw_t, gmat)
    nbk = tp // ATT_TQ
    k5 = k_p.astype(BF16).reshape(bp, nbk, ATT_TQ, N_KV_HEADS, HEAD_DIM)
    v5 = v_p.astype(BF16).reshape(bp, nbk, ATT_TQ, N_KV_HEADS, HEAD_DIM)
    ones_rows = jnp.zeros((HEAD_DIM, ATT_TQ), BF16).at[0:3].set(1)
    kt_att = jnp.concatenate([k5.transpose(0, 3, 1, 4, 2),
                              jnp.broadcast_to(ones_rows, (bp, N_KV_HEADS, nbk, HEAD_DIM, ATT_TQ))], axis=3)
    v_att = v5.transpose(0, 3, 1, 2, 4)
    b1 = nbias.astype(BF16)
    b2 = (nbias - b1.astype(F32)).astype(BF16)
    b3 = (nbias - b1.astype(F32) - b2.astype(F32)).astype(BF16)
    nb3 = jnp.zeros((N_Q_HEADS, HEAD_DIM), F32).at[:, 0:3].set(jnp.stack([b1, b2, b3], axis=1).astype(F32))
    attn_p = attn_prompt(q_p.reshape(bp, tp, ATTN_WIDTH), nb3.reshape(N_KV_HEADS, Q_PER_KV, HEAD_DIM), kt_att, v_att,
                         _suffix_sum_matrix(ATT_TQ)).reshape(bp * tp, ATTN_WIDTH)
    y_p, ssm_p, conv_p8 = ssd_mixer(proj_p, bp, tp // SSD_CHUNK, SSD_CHUNK, cw, cb, dtb, alog, dsk, snw, tri)
    x1_p = merge_out(attn_p, y_p, proj_p, xp2, wpa, wpb, wo)
    out_p = ffn(x1_p, n2w, wg, wu, wd).reshape(bp, tp, D_MODEL)

    xs2 = x_sample.reshape(bd * td, D_MODEL)
    proj_s = in_proj(xs2, n1w, w_perm)
    q_s, k_s, v_s = qkv_post(proj_s, qw_t, kw_t, gmat)
    m = N_Q_HEADS * td
    q_t = q_s.reshape(bd, td, N_Q_HEADS, HEAD_DIM).transpose(0, 2, 1, 3)
    onehot = jnp.asarray(np.arange(N_Q_HEADS)[:, None] // Q_PER_KV == np.arange(N_KV_HEADS)[None, :], BF16)
    q_pad = (q_t[:, :, :, None, :] * onehot[None, :, None, :, None]).reshape(bd, m, KV_WIDTH)
    nbias_rep = jnp.broadcast_to(jnp.repeat(nbias, td)[:, None], (m, 2 * PAGE_SIZE))
    lane_pad = ((0, 0), (0, 0), (0, PAGE_SIZE - td))
    knt = jnp.pad(k_s.reshape(bd, td, KV_WIDTH).transpose(0, 2, 1), lane_pad)
    vnt = jnp.pad(v_s.reshape(bd, td, KV_WIDTH).transpose(0, 2, 1), lane_pad)
    ckt = cache_k[0].transpose(0, 2, 3, 1)
    cvt = cache_v[0].transpose(0, 2, 3, 1)
    o_s = attn_sample(page_table.reshape(-1), q_pad, nbias_rep, knt, vnt, ckt, cvt,
                      _suffix_sum_matrix(2 * PAGE_SIZE), _suffix_sum_matrix(PAGE_SIZE), n_pages, td)
    o_s = o_s.reshape(bd, N_KV_HEADS, Q_PER_KV, td, N_KV_HEADS, HEAD_DIM)
    attn_s = jnp.einsum('bgjtgd->btgjd', o_s).reshape(bd * td, ATTN_WIDTH).astype(BF16)
    cs8 = jnp.pad(state_conv[0], ((0, 0), (8 - (CONV_W - 1), 0), (0, 0)))
    y_s, ssm_s, conv_s8 = ssd_mixer(proj_s, bd, 1, td, cw, cb, dtb, alog, dsk, snw, tri, cs8, state_ssm[0])
    x1_s = merge_out(attn_s, y_s, proj_s, xs2, wpa, wpb, wo)
    out_s = ffn(x1_s, n2w, wg, wu, wd).reshape(bd, td, D_MODEL)

    kvp = (1, bp, tp, N_KV_HEADS, HEAD_DIM)
    kvs = (1, bd, td, N_KV_HEADS, HEAD_DIM)
    return (out_p, out_s, k_p.reshape(kvp), v_p.reshape(kvp), ssm_p[None], conv_p8[None, :, 8 - (CONV_W - 1):, :],
            k_s.reshape(kvs), v_s.reshape(kvs), ssm_s[None], conv_s8[None, :, 8 - (CONV_W - 1):, :])
```
